```python
import math
import functools
import jax
import jax.numpy as jnp
from jax import lax
import numpy as np

D_MODEL = 2048
BATCH = 4
SEQ = 2048
DEPTH = 2

GRID_W = 64
CTX_LEN = 256
EPS = 1e-6
CHUNK = 64
HEAD_DIM = 128

S5_WIDTH = D_MODEL // 4
S5_GROUP = 16
S5_GROUPS = S5_WIDTH // S5_GROUP
S5_STATE = 64

GDN_WIDTH = 3 * D_MODEL // 8
GDN_HEADS = GDN_WIDTH // HEAD_DIM
GDN_DK = HEAD_DIM
GDN_DV = HEAD_DIM
GDN_CONV = 3

MLSTM_WIDTH = D_MODEL - S5_WIDTH - GDN_WIDTH
MLSTM_HEADS = MLSTM_WIDTH // HEAD_DIM
MLSTM_DK = HEAD_DIM // 2
MLSTM_DV = HEAD_DIM

FFN_HIDDEN = -(-8 * D_MODEL // (3 * 256)) * 256

IN_SIZES = (S5_WIDTH,
            GDN_HEADS * GDN_DK, GDN_HEADS * GDN_DK, GDN_HEADS * GDN_DV, GDN_HEADS * GDN_DV,
            2 * GDN_HEADS, 2 * GDN_HEADS,
            MLSTM_HEADS * MLSTM_DK, MLSTM_HEADS * MLSTM_DK, MLSTM_HEADS * MLSTM_DV, MLSTM_HEADS * MLSTM_DV,
            2 * MLSTM_HEADS, 2 * MLSTM_HEADS)
IN_COLS = sum(IN_SIZES)

kernel_name = 'hybrid_s5_gdn_mlstm_prefix_backbone'

F32 = jnp.float32


def rms_norm(x, g):
    xf = x.astype(F32)
    y = xf * lax.rsqrt(jnp.mean(xf * xf, axis=-1, keepdims=True) + EPS)
    return (y * g.astype(F32)).astype(x.dtype)


def l2norm(x):
    return x * lax.rsqrt(jnp.sum(x * x, axis=-1, keepdims=True) + EPS)


def modulate(x, g, shift, scale):
    return rms_norm(x, g) * (1 + scale) + shift


def swiglu(h, w_gate, w_up, w_down):
    return (jax.nn.silu(h @ w_gate) * (h @ w_up)) @ w_down


def split_cols(p):
    idx, off = [], 0
    for s in IN_SIZES[:-1]:
        off += s
        idx.append(off)
    return jnp.split(p, idx, axis=-1)


def short_conv(x, w):
    x = x.astype(F32)
    w = w.astype(F32)
    k = w.shape[0]
    pad = k // 2
    t = x.shape[1]
    xp = jnp.pad(x, ((0, 0), (pad, pad), (0, 0)))
    y = xp[:, 0:t] * w[0]
    for j in range(1, k):
        y = y + xp[:, j:j + t] * w[j]
    return jax.nn.silu(y)


def to_col_major(a, rows):
    b = a.shape[0]
    a = a.reshape(b, rows, GRID_W, *a.shape[2:])
    a = jnp.swapaxes(a, 1, 2)
    return a.reshape(b, rows * GRID_W, *a.shape[3:])


def from_col_major(a, rows):
    b = a.shape[0]
    a = a.reshape(b, GRID_W, rows, *a.shape[2:])
    a = jnp.swapaxes(a, 1, 2)
    return a.reshape(b, rows * GRID_W, *a.shape[3:])


def to_chunks(a):
    b, t, h = a.shape[:3]
    a = a.reshape(b, t // CHUNK, CHUNK, h, *a.shape[3:])
    return jnp.moveaxis(a, 3, 1)


def from_chunks(a):
    b, h, n, c = a.shape[:4]
    return jnp.moveaxis(a, 1, 3).reshape(b, n * c, h, *a.shape[4:])


def flip_time(tup):
    return tuple(jnp.flip(a, axis=1) for a in tup)


def bidirectional(run_f, run_b, ctx_f, lat_f, ctx_b, lat_b, state0):
    y_ctx_f, st_f = run_f(ctx_f, state0)
    y_lat_f, _ = run_f(lat_f, st_f)
    y_ctx_b, st_b = run_b(flip_time(ctx_b), state0)
    y_lat_b, _ = run_b(flip_time(lat_b), st_b)
    return y_ctx_f + jnp.flip(y_ctx_b, axis=1), y_lat_f + jnp.flip(y_lat_b, axis=1)


def s5_discretise(lam_re, lam_im, log_dt, b_re, b_im):
    dt = jnp.exp(log_dt)[:, None]
    mag = jnp.exp(lam_re * dt)
    ab_re = mag * jnp.cos(lam_im * dt)
    ab_im = mag * jnp.sin(lam_im * dt)
    den = lam_re * lam_re + lam_im * lam_im
    f_re = ((ab_re - 1.0) * lam_re + ab_im * lam_im) / den
    f_im = (ab_im * lam_re - (ab_re - 1.0) * lam_im) / den
    bb_re = f_re[..., None] * b_re - f_im[..., None] * b_im
    bb_im = f_re[..., None] * b_im + f_im[..., None] * b_re
    return ab_re, ab_im, bb_re, bb_im


def complex_affine_combine(e1, e2):
    a1r, a1i, b1r, b1i = e1
    a2r, a2i, b2r, b2i = e2
    return (a2r * a1r - a2i * a1i, a2r * a1i + a2i * a1r,
            a2r * b1r - a2i * b1i + b2r, a2r * b1i + a2i * b1r + b2i)


def s5_run(inputs, state0, par):
    (u,) = inputs
    ab_re, ab_im, bb_re, bb_im, c_re, c_im = par
    bu_re = jnp.einsum('gph,btgh->btgp', bb_re, u)
    bu_im = jnp.einsum('gph,btgh->btgp', bb_im, u)
    a_re = jnp.broadcast_to(ab_re, bu_re.shape)
    a_im = jnp.broadcast_to(ab_im, bu_im.shape)
    ar, ai, sr, si = lax.associative_scan(complex_affine_combine, (a_re, a_im, bu_re, bu_im), axis=1)
    s0r, s0i = state0
    s_re = sr + ar * s0r[:, None] - ai * s0i[:, None]
    s_im = si + ar * s0i[:, None] + ai * s0r[:, None]
    y = jnp.einsum('ghp,btgp->btgh', c_re, s_re) - jnp.einsum('ghp,btgp->btgh', c_im, s_im)
    return y, (s_re[:, -1], s_im[:, -1])


def s5_mixer(u_ctx, u_lat, lam_re, lam_im, log_dt, b_re, b_im, c_re, c_im, d, glu_w, glu_b):
    def par(i):
        ab_re, ab_im, bb_re, bb_im = s5_discretise(lam_re[i].astype(F32), lam_im[i].astype(F32),
                                                   log_dt[i].astype(F32), b_re[i].astype(F32),
                                                   b_im[i].astype(F32))
        return (ab_re, ab_im, bb_re, bb_im, c_re[i].astype(F32), c_im[i].astype(F32))

    def grp(u):
        return u.astype(F32).reshape(u.shape[0], u.shape[1], S5_GROUPS, S5_GROUP)

    uc, ul = grp(u_ctx), grp(u_lat)
    zeros = jnp.zeros((u_lat.shape[0], S5_GROUPS, S5_STATE), F32)
    run_f = functools.partial(s5_run, par=par(0))
    run_b = functools.partial(s5_run, par=par(1))
    y_ctx, y_lat = bidirectional(run_f, run_b, (uc,), (ul,), (uc,), (ul,), (zeros, zeros))

    def out(y, u):
        y = jax.nn.gelu(y + d.astype(F32) * u).reshape(u.shape[0], u.shape[1], S5_WIDTH)
        return y * jax.nn.sigmoid(y @ glu_w.astype(F32) + glu_b.astype(F32))

    return out(y_ctx, uc), out(y_lat, ul)


def gdn_run(inputs, state0):
    q, k, v, g, beta = (to_chunks(a) for a in inputs)
    tri = jnp.tril(jnp.ones((CHUNK, CHUNK), dtype=bool))
    strict = jnp.tril(jnp.ones((CHUNK, CHUNK), dtype=bool), -1)
    eye = jnp.eye(CHUNK, dtype=F32)
    gc = jnp.cumsum(g, axis=-1)
    diff = gc[..., :, None] - gc[..., None, :]
    decay = jnp.where(tri, jnp.exp(jnp.where(tri, diff, 0.0)), 0.0)
    kb = k * beta[..., None]
    vb = v * beta[..., None]
    m = jnp.where(strict, jnp.einsum('bhnid,bhnjd->bhnij', kb, k) * decay, 0.0)
    t_inv = lax.linalg.triangular_solve(eye + m, jnp.broadcast_to(eye, m.shape),
                                        left_side=True, lower=True, unit_diagonal=True)
    u = t_inv @ vb
    w = t_inv @ (kb * jnp.exp(gc)[..., None])
    qk = jnp.where(tri, jnp.einsum('bhnid,bhnjd->bhnij', q, k) * decay, 0.0)
    q_dec = q * jnp.exp(gc)[..., None]
    k_dec = k * jnp.exp(gc[..., -1:] - gc)[..., None]
    g_tot = jnp.exp(gc[..., -1])

    def step(state, xs):
        w_c, u_c, qk_c, qd_c, kd_c, gt_c = xs
        v_new = u_c - w_c @ state
        o = qd_c @ state + qk_c @ v_new
        state = state * gt_c[..., None, None] + jnp.einsum('bhcd,bhce->bhde', kd_c, v_new)
        return state, o

    xs = tuple(jnp.moveaxis(a, 2, 0) for a in (w, u, qk, q_dec, k_dec, g_tot))
    s_fin, o = lax.scan(step, state0, xs)
    return from_chunks(jnp.moveaxis(o, 0, 2)), s_fin


def gdn_prepare(q, k, v, a, b, conv_w, a_log, dt_bias):
    bsz, t = q.shape[:2]
    qkv = short_conv(jnp.concatenate([q, k, v], axis=-1), conv_w)
    q, k, v = jnp.split(qkv, [GDN_HEADS * GDN_DK, 2 * GDN_HEADS * GDN_DK], axis=-1)
    q = l2norm(q.reshape(bsz, t, GDN_HEADS, GDN_DK)) * GDN_DK ** -0.5
    k = l2norm(k.reshape(bsz, t, GDN_HEADS, GDN_DK))
    v = v.reshape(bsz, t, GDN_HEADS, GDN_DV)
    a = a.astype(F32).reshape(bsz, t, 2, GDN_HEADS)
    b = b.astype(F32).reshape(bsz, t, 2, GDN_HEADS)
    g = -jnp.exp(a_log.astype(F32)) * jax.nn.softplus(a + dt_bias.astype(F32))
    beta = jax.nn.sigmoid(b)
    return (q, k, v, g[:, :, 0], beta[:, :, 0]), (q, k, v, g[:, :, 1], beta[:, :, 1])


def gdn_mixer(ctx_parts, lat_parts, conv_w, a_log, dt_bias, norm):
    cq, ck, cv, cz, ca, cb = ctx_parts
    lq, lk, lv, lz, la, lb = lat_parts
    c_f, c_b = gdn_prepare(cq, ck, cv, ca, cb, conv_w, a_log, dt_bias)
    l_f, l_b = gdn_prepare(lq, lk, lv, la, lb, conv_w, a_log, dt_bias)
    s0 = jnp.zeros((lq.shape[0], GDN_HEADS, GDN_DK, GDN_DV), F32)
    o_ctx, o_lat = bidirectional(gdn_run, gdn_run, c_f, l_f, c_b, l_b, s0)

    def gate_out(o, z):
        z = z.astype(F32).reshape(o.shape)
        return (rms_norm(o, norm) * jax.nn.silu(z)).reshape(o.shape[0], o.shape[1], GDN_WIDTH)

    return gate_out(o_ctx, cz), gate_out(o_lat, lz)


def mlstm_run(inputs, state0):
    q, k, v, ig, lf = (to_chunks(a) for a in inputs)
    tri = jnp.tril(jnp.ones((CHUNK, CHUNK), dtype=bool))
    b = jnp.cumsum(lf, axis=-1)
    log_d = jnp.where(tri, b[..., :, None] - b[..., None, :] + ig[..., None, :], -jnp.inf)
    m_intra = jnp.max(log_d, axis=-1)
    s = jnp.einsum('bhnid,bhnjd->bhnij', q, k) * jnp.exp(log_d - m_intra[..., None])
    num_intra = s @ v
    den_intra = jnp.sum(s, axis=-1)
    log_w = b[..., -1:] - b + ig
    m_chunk = jnp.max(log_w, axis=-1)
    wk = k * jnp.exp(log_w - m_chunk[..., None])[..., None]
    kv_chunk = jnp.einsum('bhncd,bhnce->bhnde', wk, v)
    k_chunk = jnp.sum(wk, axis=-2)
    b_tot = b[..., -1]

    def step(carry, xs):
        c_st, n_st, m_st = carry
        q_c, b_c, mi_c, ni_c, di_c, kv_c, kc_c, bt_c, mc_c = xs
        m_t = jnp.maximum(b_c + m_st[..., None], mi_c)
        inter = jnp.exp(b_c + m_st[..., None] - m_t)
        intra = jnp.exp(mi_c - m_t)
        num = inter[..., None] * (q_c @ c_st) + intra[..., None] * ni_c
        den = inter * jnp.einsum('bhcd,bhd->bhc', q_c, n_st) + intra * di_c
        h = num / jnp.maximum(jnp.abs(den), jnp.exp(-m_t))[..., None]
        m_new = jnp.maximum(bt_c + m_st, mc_c)
        a_old = jnp.exp(bt_c + m_st - m_new)
        a_new = jnp.exp(mc_c - m_new)
        c_st = a_old[..., None, None] * c_st + a_new[..., None, None] * kv_c
        n_st = a_old[..., None] * n_st + a_new[..., None] * kc_c
        return (c_st, n_st, m_new), h

    xs = tuple(jnp.moveaxis(a, 2, 0) for a in
               (q, b, m_intra, num_intra, den_intra, kv_chunk, k_chunk, b_tot, m_chunk))
    state, h = lax.scan(step, state0, xs)
    return from_chunks(jnp.moveaxis(h, 0, 2)), state


def mlstm_prepare(q, k, v, i, f, i_bias, f_bias):
    bsz, t = q.shape[:2]
    q = q.astype(F32).reshape(bsz, t, MLSTM_HEADS, MLSTM_DK) * MLSTM_DK ** -0.5
    k = k.astype(F32).reshape(bsz, t, MLSTM_HEADS, MLSTM_DK)
    v = v.astype(F32).reshape(bsz, t, MLSTM_HEADS, MLSTM_DV)
    ig = i.astype(F32).reshape(bsz, t, 2, MLSTM_HEADS) + i_bias.astype(F32)
    lf = jax.nn.log_sigmoid(f.astype(F32).reshape(bsz, t, 2, MLSTM_HEADS) + f_bias.astype(F32))
    return (q, k, v, ig[:, :, 0], lf[:, :, 0]), (q, k, v, ig[:, :, 1], lf[:, :, 1])


def mlstm_mixer(ctx_parts, lat_parts, rows, i_bias, f_bias, norm):
    cq, ck, cv, co, ci, cf = ctx_parts
    lq, lk, lv, lo, li, lf = lat_parts
    c_f, c_b = mlstm_prepare(cq, ck, cv, ci, cf, i_bias, f_bias)
    l_f, l_b = mlstm_prepare(lq, lk, lv, li, lf, i_bias, f_bias)
    l_f = tuple(to_col_major(a, rows) for a in l_f)
    l_b = tuple(to_col_major(a, rows) for a in l_b)
    bsz = lq.shape[0]
    s0 = (jnp.zeros((bsz, MLSTM_HEADS, MLSTM_DK, MLSTM_DV), F32),
          jnp.zeros((bsz, MLSTM_HEADS, MLSTM_DK), F32),
          jnp.zeros((bsz, MLSTM_HEADS), F32))
    h_ctx, h_lat = bidirectional(mlstm_run, mlstm_run, c_f, l_f, c_b, l_b, s0)
    h_lat = from_col_major(h_lat, rows)

    def gate_out(h, o):
        o = o.astype(F32).reshape(h.shape)
        return (rms_norm(h, norm) * jax.nn.sigmoid(o)).reshape(h.shape[0], h.shape[1], MLSTM_WIDTH)

    return gate_out(h_ctx, co), gate_out(h_lat, lo)


def token_mixer(h_ctx, h_lat, rows, w_in, s5_lam_re, s5_lam_im, s5_log_dt, s5_b_re, s5_b_im,
                s5_c_re, s5_c_im, s5_d, s5_glu_w, s5_glu_b, gdn_conv_w, gdn_a_log, gdn_dt_bias,
                gdn_norm, mlstm_i_bias, mlstm_f_bias, mlstm_norm):
    pc = split_cols(h_ctx @ w_in)
    pl = split_cols(h_lat @ w_in)
    s5_c, s5_l = s5_mixer(pc[0], pl[0], s5_lam_re, s5_lam_im, s5_log_dt, s5_b_re, s5_b_im,
                          s5_c_re, s5_c_im, s5_d, s5_glu_w, s5_glu_b)
    gdn_c, gdn_l = gdn_mixer(pc[1:7], pl[1:7], gdn_conv_w, gdn_a_log, gdn_dt_bias, gdn_norm)
    ml_c, ml_l = mlstm_mixer(pc[7:13], pl[7:13], rows, mlstm_i_bias, mlstm_f_bias, mlstm_norm)
    mix_ctx = jnp.concatenate([s5_c, gdn_c, ml_c], axis=-1).astype(h_ctx.dtype)
    mix_lat = jnp.concatenate([s5_l, gdn_l, ml_l], axis=-1).astype(h_lat.dtype)
    return mix_ctx, mix_lat


def setup_inputs(seed: int = 0) -> dict:
    key = jax.random.key(seed)
    ks = jax.random.split(key, 32)
    L, D, G, P, Hg = DEPTH, D_MODEL, S5_GROUPS, S5_STATE, S5_GROUP

    def nrm(i, shape, std):
        return std * jax.random.normal(ks[i], shape, F32)

    def gain(i, shape):
        return 1.0 + 0.02 * jax.random.normal(ks[i], shape, F32)

    dt_gdn = jnp.exp(jax.random.uniform(ks[23], (L, 2, GDN_HEADS), F32, math.log(1e-3), math.log(1e-1)))
    return {
        'x': nrm(0, (BATCH, SEQ, D), 1.0),
        'c': nrm(1, (BATCH, D), 1.0),
        'ctx': nrm(2, (BATCH, CTX_LEN, D), 1.0),
        'c_ctx': nrm(3, (D,), 1.0),
        'ada_w': nrm(4, (L, D, 6 * D), 0.5 * D ** -0.5),
        'ada_b': nrm(5, (L, 6 * D), 0.01),
        'norm_mix_pre': gain(6, (L, D)),
        'norm_mix_post': gain(7, (L, D)),
        'norm_ffn_pre': gain(8, (L, D)),
        'norm_ffn_post': gain(9, (L, D)),
        'w_in': nrm(10, (L, D, IN_COLS), D ** -0.5),
        'w_out': nrm(11, (L, D, D), D ** -0.5),
        's5_lam_re': -0.5 + nrm(12, (L, 2, G, P), 0.01),
        's5_lam_im': math.pi * jnp.arange(P, dtype=F32) + nrm(13, (L, 2, G, P), 0.01),
        's5_log_dt': jax.random.uniform(ks[14], (L, 2, G), F32, math.log(1e-3), math.log(1e-1)),
        's5_b_re': nrm(15, (L, 2, G, P, Hg), (2 * Hg) ** -0.5),
        's5_b_im': nrm(16, (L, 2, G, P, Hg), (2 * Hg) ** -0.5),
        's5_c_re': nrm(17, (L, 2, G, Hg, P), P ** -0.5),
        's5_c_im': nrm(18, (L, 2, G, Hg, P), P ** -0.5),
        's5_d': nrm(19, (L, G, Hg), 1.0),
        's5_glu_w': nrm(20, (L, S5_WIDTH, S5_WIDTH), S5_WIDTH ** -0.5),
        's5_glu_b': nrm(21, (L, S5_WIDTH), 0.01),
        'gdn_conv_w': nrm(22, (L, GDN_CONV, 2 * GDN_HEADS * GDN_DK + GDN_HEADS * GDN_DV), GDN_CONV ** -0.5),
        'gdn_a_log': jnp.log(jax.random.uniform(ks[24], (L, 2, GDN_HEADS), F32, 1.0, 16.0)),
        'gdn_dt_bias': dt_gdn + jnp.log(-jnp.expm1(-dt_gdn)),
        'gdn_norm': gain(25, (L, GDN_DV)),
        'mlstm_i_bias': nrm(26, (L, 2, MLSTM_HEADS), 0.1),
        'mlstm_f_bias': jnp.linspace(3.0, 6.0, MLSTM_HEADS, dtype=F32) + nrm(27, (L, 2, MLSTM_HEADS), 0.01),
        'mlstm_norm': gain(28, (L, MLSTM_DV)),
        'ffn_w_gate': nrm(29, (L, D, FFN_HIDDEN), D ** -0.5),
        'ffn_w_up': nrm(30, (L, D, FFN_HIDDEN), D ** -0.5),
        'ffn_w_down': nrm(31, (L, FFN_HIDDEN, D), FFN_HIDDEN ** -0.5),
    }


def reference(x, c, ctx, c_ctx, ada_w, ada_b, norm_mix_pre, norm_mix_post, norm_ffn_pre, norm_ffn_post,
              w_in, w_out, s5_lam_re, s5_lam_im, s5_log_dt, s5_b_re, s5_b_im, s5_c_re, s5_c_im, s5_d,
              s5_glu_w, s5_glu_b, gdn_conv_w, gdn_a_log, gdn_dt_bias, gdn_norm, mlstm_i_bias,
              mlstm_f_bias, mlstm_norm, ffn_w_gate, ffn_w_up, ffn_w_down):
    rows = x.shape[1] // GRID_W
    x_lat, x_ctx = x, ctx
    for l in range(DEPTH):
        mods = jnp.split(jax.nn.silu(c) @ ada_w[l] + ada_b[l], 6, axis=-1)
        sh_m, sc_m, gt_m, sh_f, sc_f, gt_f = (m[:, None, :] for m in mods)
        c_sh_m, c_sc_m, c_gt_m, c_sh_f, c_sc_f, c_gt_f = jnp.split(
            jax.nn.silu(c_ctx) @ ada_w[l] + ada_b[l], 6, axis=-1)

        h_lat = modulate(x_lat, norm_mix_pre[l], sh_m, sc_m)
        h_ctx = modulate(x_ctx, norm_mix_pre[l], c_sh_m, c_sc_m)
        mix_ctx, mix_lat = token_mixer(
            h_ctx, h_lat, rows, w_in[l], s5_lam_re[l], s5_lam_im[l], s5_log_dt[l], s5_b_re[l],
            s5_b_im[l], s5_c_re[l], s5_c_im[l], s5_d[l], s5_glu_w[l], s5_glu_b[l], gdn_conv_w[l],
            gdn_a_log[l], gdn_dt_bias[l], gdn_norm[l], mlstm_i_bias[l], mlstm_f_bias[l], mlstm_norm[l])

        x_lat = x_lat + gt_m * rms_norm(mix_lat @ w_out[l], norm_mix_post[l])
        f_lat = swiglu(modulate(x_lat, norm_ffn_pre[l], sh_f, sc_f), ffn_w_gate[l], ffn_w_up[l], ffn_w_down[l])
        x_lat = x_lat + gt_f * rms_norm(f_lat, norm_ffn_post[l])

        if l < DEPTH - 1:
            x_ctx = x_ctx + c_gt_m * rms_norm(mix_ctx @ w_out[l], norm_mix_post[l])
            f_ctx = swiglu(modulate(x_ctx, norm_ffn_pre[l], c_sh_f, c_sc_f),
                           ffn_w_gate[l], ffn_w_up[l], ffn_w_down[l])
            x_ctx = x_ctx + c_gt_f * rms_norm(f_ctx, norm_ffn_post[l])
    return x_lat
```

```python
import functools
import math

import jax
import jax.numpy as jnp
from jax import lax
from jax.experimental import pallas as pl
from jax.experimental.pallas import tpu as pltpu

F32 = jnp.float32
BF16 = jnp.bfloat16

EPS = 1e-6
GRID_W = 64
CHUNK = 64
SUPER = 4 * CHUNK
HEAD_DIM = 128
S5_GROUP = 16
S5_STATE = 64
S5_CHUNK = 16
S5_ROW = S5_CHUNK * S5_GROUP
SUBLANES = 8
ML_DK = 64

VMEM_LIMIT = 48 * 1024 * 1024


def _cparams(sem):
    return pltpu.CompilerParams(dimension_semantics=sem, vmem_limit_bytes=VMEM_LIMIT)


def _bdot(a, b, dims=((1,), (0,))):
    return lax.dot_general(a.astype(BF16), b.astype(BF16), (dims, ((), ())),
                           preferred_element_type=F32)


def _split3(a):
    a0 = a.astype(BF16)
    r1 = a - a0.astype(F32)
    a1 = r1.astype(BF16)
    a2 = (r1 - a1.astype(F32)).astype(BF16)
    return a0, a1, a2


def _dot3(a, b):
    a0, a1, _ = _split3(a)
    b0, b1, _ = _split3(b)
    d = functools.partial(jnp.dot, preferred_element_type=F32)
    return d(a0, b0) + (d(a0, b1) + d(a1, b0))


def _dot_exact_rhs(a, b01):
    a0, a1, a2 = _split3(a)
    b = b01.astype(BF16)
    d = functools.partial(jnp.dot, preferred_element_type=F32)
    return d(a0, b) + (d(a1, b) + d(a2, b))


def _sigmoid(x):
    return 1.0 / (1.0 + jnp.exp(-x))


def _silu(x):
    return x * _sigmoid(x)


def _softplus(x):
    return jnp.maximum(x, 0.0) + jnp.log(1.0 + jnp.exp(-jnp.abs(x)))


def _rms(x, g):
    return x * lax.rsqrt(jnp.mean(x * x, axis=-1, keepdims=True) + EPS) * g


def _ada_kernel(c_ref, w_ref, b_ref, o_ref):
    c = c_ref[...]
    o_ref[...] = _bdot(_silu(c), w_ref[...]) + b_ref[...]


def _ada(c8, ada_w, ada_b):
    L, D, N = ada_w.shape
    tn = 1024
    return pl.pallas_call(
        _ada_kernel,
        grid=(L, N // tn),
        in_specs=[pl.BlockSpec((SUBLANES, D), lambda l, j: (0, 0)),
                  pl.BlockSpec((None, D, tn), lambda l, j: (l, 0, j)),
                  pl.BlockSpec((None, 1, tn), lambda l, j: (l, 0, j))],
        out_specs=pl.BlockSpec((None, SUBLANES, tn), lambda l, j: (l, 0, j)),
        out_shape=jax.ShapeDtypeStruct((L, SUBLANES, N), F32),
        compiler_params=_cparams(("parallel", "arbitrary")),
        name="ada",
    )(c8, ada_w, ada_b.reshape(L, 1, N))


def _inproj_kernel(x_ref, sh_ref, sc_ref, g_ref, w_ref, o_ref, h_scr):
    @pl.when(pl.program_id(1) == 0)
    def _():
        h = _rms(x_ref[...], g_ref[...]) * (1.0 + sc_ref[0]) + sh_ref[0]
        h_scr[...] = h.astype(BF16)

    o_ref[...] = jnp.dot(h_scr[...], w_ref[...], preferred_element_type=F32)


def _inproj(x, mods, mod_row, g_pre, w, tm, tn):
    M, D = x.shape
    N = w.shape[1]
    return pl.pallas_call(
        _inproj_kernel,
        grid=(M // tm, N // tn),
        in_specs=[pl.BlockSpec((tm, D), lambda i, j: (i, 0)),
                  pl.BlockSpec((1, 1, D), lambda i, j: (mod_row(i), 0, 0)),
                  pl.BlockSpec((1, 1, D), lambda i, j: (mod_row(i), 0, 1)),
                  pl.BlockSpec((1, D), lambda i, j: (0, 0)),
                  pl.BlockSpec((D, tn), lambda i, j: (0, j))],
        out_specs=pl.BlockSpec((tm, tn), lambda i, j: (i, j)),
        out_shape=jax.ShapeDtypeStruct((M, N), F32),
        scratch_shapes=[pltpu.VMEM((tm, D), BF16)],
        compiler_params=_cparams(("parallel", "arbitrary")),
        name="inproj",
    )(x, mods, mods, g_pre, w)


def _outproj_kernel(x_ref, a_ref, b_ref, c_ref, gt_ref, g_ref, wa_ref, wb_ref, wc_ref, o_ref):
    acc = jnp.dot(a_ref[...], wa_ref[...], preferred_element_type=F32)
    acc += jnp.dot(b_ref[...], wb_ref[...], preferred_element_type=F32)
    acc += jnp.dot(c_ref[...], wc_ref[...], preferred_element_type=F32)
    o_ref[...] = x_ref[...] + gt_ref[0] * _rms(acc, g_ref[...])


def _outproj(x, ma, mb, mc, mods, mod_row, g_post, wa, wb, wc, tm):
    M, D = x.shape
    row = lambda i: (i, 0)
    full = lambda i: (0, 0)
    return pl.pallas_call(
        _outproj_kernel,
        grid=(M // tm,),
        in_specs=[pl.BlockSpec((tm, D), row),
                  pl.BlockSpec((tm, ma.shape[1]), row),
                  pl.BlockSpec((tm, mb.shape[1]), row),
                  pl.BlockSpec((tm, mc.shape[1]), row),
                  pl.BlockSpec((1, 1, D), lambda i: (mod_row(i), 0, 2)),
                  pl.BlockSpec((1, D), full),
                  pl.BlockSpec(wa.shape, full),
                  pl.BlockSpec(wb.shape, full),
                  pl.BlockSpec(wc.shape, full)],
        out_specs=pl.BlockSpec((tm, D), row),
        out_shape=jax.ShapeDtypeStruct((M, D), F32),
        compiler_params=_cparams(("parallel",)),
        name="outproj",
    )(x, ma, mb, mc, mods, g_post, wa, wb, wc)


def _ffn_kernel(x_ref, sh_ref, sc_ref, gt_ref, gpre_ref, gpost_ref, wg_ref, wu_ref, wd_ref,
                o_ref, h_scr, acc_scr):
    j = pl.program_id(1)

    @pl.when(j == 0)
    def _():
        h = _rms(x_ref[...], gpre_ref[...]) * (1.0 + sc_ref[0]) + sh_ref[0]
        h_scr[...] = h.astype(BF16)
        acc_scr[...] = jnp.zeros_like(acc_scr)

    h = h_scr[...]
    g = jnp.dot(h, wg_ref[...], preferred_element_type=F32)
    u = jnp.dot(h, wu_ref[...], preferred_element_type=F32)
    a = (_silu(g) * u).astype(BF16)
    acc_scr[...] += jnp.dot(a, wd_ref[...], preferred_element_type=F32)

    @pl.when(j == pl.num_programs(1) - 1)
    def _():
        o_ref[...] = x_ref[...] + gt_ref[0] * _rms(acc_scr[...], gpost_ref[...])


def _ffn(x, mods, mod_row, g_pre, g_post, wg, wu, wd, tm, th):
    M, D = x.shape
    H = wg.shape[1]
    row = lambda i, j: (i, 0)
    full = lambda i, j: (0, 0)
    return pl.pallas_call(
        _ffn_kernel,
        grid=(M // tm, H // th),
        in_specs=[pl.BlockSpec((tm, D), row),
                  pl.BlockSpec((1, 1, D), lambda i, j: (mod_row(i), 0, 3)),
                  pl.BlockSpec((1, 1, D), lambda i, j: (mod_row(i), 0, 4)),
                  pl.BlockSpec((1, 1, D), lambda i, j: (mod_row(i), 0, 5)),
                  pl.BlockSpec((1, D), full),
                  pl.BlockSpec((1, D), full),
                  pl.BlockSpec((D, th), lambda i, j: (0, j)),
                  pl.BlockSpec((D, th), lambda i, j: (0, j)),
                  pl.BlockSpec((th, D), lambda i, j: (j, 0))],
        out_specs=pl.BlockSpec((tm, D), row),
        out_shape=jax.ShapeDtypeStruct((M, D), F32),
        scratch_shapes=[pltpu.VMEM((tm, D), BF16), pltpu.VMEM((tm, D), F32)],
        compiler_params=_cparams(("parallel", "arbitrary")),
        name="ffn",
    )(x, mods, mods, mods, g_pre, g_post, wg, wu, wd)


def _s5_param_kernel(lre_ref, lim_ref, ldt_ref, bre_ref, bim_ref, cre_ref, cim_ref,
                     k_ref, care_ref, ncaim_ref, abre_ref, abim_ref, alre_ref, alim_ref):
    dt = jnp.exp(ldt_ref[...])
    lre, lim = lre_ref[...], lim_ref[...]
    mag = jnp.exp(lre * dt)
    ar = mag * jnp.cos(lim * dt)
    ai = mag * jnp.sin(lim * dt)
    den = lre * lre + lim * lim
    f_re = ((ar - 1.0) * lre + ai * lim) / den
    f_im = (ai * lre - (ar - 1.0) * lim) / den
    b_re, b_im = bre_ref[...], bim_ref[...]
    bb_re = f_re * b_re - f_im * b_im
    bb_im = f_re * b_im + f_im * b_re
    c_re, c_im = cre_ref[...], cim_ref[...]

    def lag_kernel(ca, bb):
        return jnp.einsum('ghp,gjp->ghj', ca, bb, precision=lax.Precision.HIGHEST,
                          preferred_element_type=F32)

    pr, pi = jnp.ones_like(ar), jnp.zeros_like(ar)
    for k in range(S5_CHUNK + 1):
        ca_re = c_re * pr - c_im * pi
        ca_im = c_re * pi + c_im * pr
        care_ref[k] = ca_re
        ncaim_ref[k] = -ca_im
        if k < S5_CHUNK:
            k_ref[k] = lag_kernel(ca_re, bb_re) - lag_kernel(ca_im, bb_im)
            abre_ref[k] = pr * bb_re - pi * bb_im
            abim_ref[k] = pr * bb_im + pi * bb_re
        else:
            alre_ref[...] = pr
            alim_ref[...] = pi
        pr, pi = pr * ar - pi * ai, pr * ai + pi * ar


def _s5_params(lam_re, lam_im, log_dt, bt_re, bt_im, c_re, c_im):
    _, G, P = lam_re.shape
    Hg = bt_re.shape[-2]
    Lc = S5_CHUNK
    gb = 8
    par = lambda *s: pl.BlockSpec((None, gb) + s, lambda d, g: (d, g) + (0,) * len(s))
    lag = lambda n, *s: pl.BlockSpec((None, n, gb) + s, lambda d, g: (d, 0, g) + (0,) * len(s))
    return pl.pallas_call(
        _s5_param_kernel,
        grid=(2, G // gb),
        in_specs=[par(1, P), par(1, P), par(1, 1), par(Hg, P), par(Hg, P), par(Hg, P), par(Hg, P)],
        out_specs=[lag(Lc, Hg, Hg), lag(Lc + 1, Hg, P), lag(Lc + 1, Hg, P),
                   lag(Lc, Hg, P), lag(Lc, Hg, P), par(1, P), par(1, P)],
        out_shape=[jax.ShapeDtypeStruct((2, Lc, G, Hg, Hg), F32),
                   jax.ShapeDtypeStruct((2, Lc + 1, G, Hg, P), F32),
                   jax.ShapeDtypeStruct((2, Lc + 1, G, Hg, P), F32),
                   jax.ShapeDtypeStruct((2, Lc, G, Hg, P), F32),
                   jax.ShapeDtypeStruct((2, Lc, G, Hg, P), F32),
                   jax.ShapeDtypeStruct((2, G, 1, P), F32),
                   jax.ShapeDtypeStruct((2, G, 1, P), F32)],
        compiler_params=_cparams(("parallel", "parallel")),
        name="s5_params",
    )(lam_re.reshape(2, G, 1, P), lam_im.reshape(2, G, 1, P), log_dt.reshape(2, G, 1, 1),
      bt_re, bt_im, c_re, c_im)


def _s5_scan_kernel(u_ref, tt_ref, ore_ref, oim_ref, wre_ref, wim_ref, al_ref, y_ref,
                    sin_re, sin_im, sst_re, sst_im, *, nc_ctx, nc_lat):
    u = u_ref[...].astype(BF16)
    nb = SUBLANES
    y = None
    for d in range(2):
        sin_re[...] = jnp.dot(u, wre_ref[d], preferred_element_type=F32)
        sin_im[...] = jnp.dot(u, wim_ref[d], preferred_element_type=F32)
        ar = al_ref[2 * d:2 * d + 1, :]
        ai = al_ref[2 * d + 1:2 * d + 2, :]

        def make_body(base, n):
            def body(i, carry):
                sr, si = carry
                c = i if d == 0 else n - 1 - i
                row = pl.multiple_of(base + c * nb, nb)
                sst_re[pl.ds(row, nb), :] = sr
                sst_im[pl.ds(row, nb), :] = si
                xr = sin_re[pl.ds(row, nb), :]
                xi = sin_im[pl.ds(row, nb), :]
                return ar * sr - ai * si + xr, ar * si + ai * sr + xi
            return body

        z = jnp.zeros((nb, S5_STATE), F32)
        carry = lax.fori_loop(0, nc_ctx, make_body(0, nc_ctx), (z, z))
        lax.fori_loop(0, nc_lat, make_body(nc_ctx * nb, nc_lat), carry)
        yd = jnp.dot(u, tt_ref[d], preferred_element_type=F32)
        yd += _bdot(sst_re[...], ore_ref[d])
        yd += _bdot(sst_im[...], oim_ref[d])
        y = yd if y is None else y + yd
    y_ref[...] = y


def _s5_scan(u, tt, ore, oim, wre, wim, al, nc_ctx, nc_lat):
    G, R, W = u.shape
    P = S5_STATE
    g3 = lambda *s: pl.BlockSpec((None,) + s, lambda g: (g,) + (0,) * len(s))
    return pl.pallas_call(
        functools.partial(_s5_scan_kernel, nc_ctx=nc_ctx, nc_lat=nc_lat),
        grid=(G,),
        in_specs=[g3(R, W), g3(2, W, W), g3(2, P, W), g3(2, P, W), g3(2, W, P), g3(2, W, P),
                  g3(4, P)],
        out_specs=g3(R, W),
        out_shape=jax.ShapeDtypeStruct((G, R, W), F32),
        scratch_shapes=[pltpu.VMEM((R, P), F32)] * 4,
        compiler_params=_cparams(("parallel",)),
        name="s5_scan",
    )(u, tt, ore, oim, wre, wim, al)


def _s5_out_kernel(y_ref, u_ref, d_ref, w_ref, b_ref, o_ref):
    x = y_ref[...] + d_ref[...] * u_ref[...]
    y = 0.5 * x * (1.0 + jnp.tanh(math.sqrt(2.0 / math.pi) * (x + 0.044715 * (x * x * x))))
    gate = _bdot(y, w_ref[...]) + b_ref[...]
    o_ref[...] = (y * _sigmoid(gate)).astype(o_ref.dtype)


def _s5_out(y, p, d, w, b, tm):
    M, W = y.shape
    row = lambda i: (i, 0)
    full = lambda i: (0, 0)
    return pl.pallas_call(
        _s5_out_kernel,
        grid=(M // tm,),
        in_specs=[pl.BlockSpec((tm, W), row), pl.BlockSpec((tm, W), row),
                  pl.BlockSpec((1, W), full), pl.BlockSpec((W, W), full),
                  pl.BlockSpec((1, W), full)],
        out_specs=pl.BlockSpec((tm, W), row),
        out_shape=jax.ShapeDtypeStruct((M, W), BF16),
        compiler_params=_cparams(("parallel",)),
        name="s5_out",
    )(y, p, d, w, b)


def _s5_mixer(p_ctx, p_lat, B, lam_re, lam_im, log_dt, b_re, b_im, c_re, c_im, d, glu_w, glu_b):
    G, Hg, P, Lc = lam_re.shape[1], S5_GROUP, S5_STATE, S5_CHUNK
    W = G * Hg
    kk, care, ncaim, abre, abim, alre, alim = _s5_params(
        lam_re, lam_im, log_dt, b_re.swapaxes(-1, -2), b_im.swapaxes(-1, -2), c_re, c_im)
    s_i = jnp.arange(Lc)[:, None]
    t_i = jnp.arange(Lc)[None, :]

    def toeplitz(k, lag):
        m = jnp.where((lag >= 0)[:, :, None, None, None], k[jnp.clip(lag, 0, Lc - 1)], 0.0)
        return m.transpose(2, 0, 4, 1, 3).reshape(G, S5_ROW, S5_ROW)

    tt = jnp.stack([toeplitz(kk[0], t_i - s_i), toeplitz(kk[1], s_i - t_i)], 1).astype(BF16)

    def readout(ca):
        f = ca[0, 1:]
        b = ca[1, :0:-1]
        o = jnp.stack([f, b], 0)
        return o.transpose(2, 0, 4, 1, 3).reshape(G, 2, P, S5_ROW).astype(BF16)

    def writein(ab):
        w = jnp.stack([ab[0, ::-1], ab[1]], 0)
        return w.transpose(2, 0, 1, 3, 4).reshape(G, 2, S5_ROW, P).astype(BF16)

    ore, oim = readout(care), readout(ncaim)
    wre, wim = writein(abre), writein(abim)
    al = jnp.stack([alre[0], alim[0], alre[1], alim[1]], 1).reshape(G, 4, P)

    def to_rows(p):
        nc = p.shape[0] // B // Lc
        u = p[:, :W].reshape(B, nc, Lc, G, Hg).transpose(3, 1, 0, 2, 4)
        u = jnp.pad(u, ((0, 0), (0, 0), (0, SUBLANES - B), (0, 0), (0, 0)))
        return u.reshape(G, nc * SUBLANES, S5_ROW), nc

    def from_rows(y, nc):
        y = y.reshape(G, nc, SUBLANES, Lc, Hg)[:, :, :B]
        return y.transpose(2, 1, 3, 0, 4).reshape(B * nc * Lc, W)

    u_c, nc_ctx = to_rows(p_ctx)
    u_l, nc_lat = to_rows(p_lat)
    y = _s5_scan(jnp.concatenate([u_c, u_l], 1), tt, ore, oim, wre, wim, al, nc_ctx, nc_lat)
    y_c = from_rows(y[:, :nc_ctx * SUBLANES], nc_ctx)
    y_l = from_rows(y[:, nc_ctx * SUBLANES:], nc_lat)
    dd = d.reshape(1, W)
    gw = glu_w.astype(BF16)
    gb = glu_b.reshape(1, W)
    return (_s5_out(y_c, p_ctx, dd, gw, gb, min(1024, y_c.shape[0])),
            _s5_out(y_l, p_lat, dd, gw, gb, min(1024, y_l.shape[0])))


def _gates_kernel(a_ref, b_ref, i_ref, f_ref, alog_ref, dtb_ref, ib_ref, fb_ref,
                  beta_ref, cum_ref, ecum_ref, edk_ref, egt_ref,
                  bc_ref, r_ref, ew_ref, bt_ref, mc_ref):
    n = a_ref.shape[0] // 2
    r_i = lax.broadcasted_iota(jnp.int32, (CHUNK, CHUNK), 0)
    c_i = lax.broadcasted_iota(jnp.int32, (CHUNK, CHUNK), 1)
    incl = [(r_i <= c_i).astype(F32), (r_i >= c_i).astype(F32)]

    def cumsum(x):
        return jnp.concatenate([_dot_exact_rhs(x[:n], incl[0]), _dot_exact_rhs(x[n:], incl[1])], 0)

    g = -jnp.exp(alog_ref[...]) * _softplus(a_ref[...] + dtb_ref[...])
    cum = cumsum(g)
    tot = jnp.sum(g, axis=-1, keepdims=True)
    beta_ref[...] = _sigmoid(b_ref[...])
    cum_ref[...] = cum
    ecum_ref[...] = jnp.exp(cum)
    edk_ref[...] = jnp.exp(tot - cum)
    egt_ref[...] = jnp.broadcast_to(jnp.exp(tot), cum.shape)

    ig = i_ref[...] + ib_ref[...]
    lf = -_softplus(-(f_ref[...] + fb_ref[...]))
    bc = cumsum(lf)
    bt = jnp.sum(lf, axis=-1, keepdims=True)
    log_w = bt - bc + ig
    mch = jnp.max(log_w, axis=-1, keepdims=True)
    bc_ref[...] = bc
    r_ref[...] = ig - bc
    ew_ref[...] = jnp.exp(log_w - mch)
    bt_ref[...] = jnp.broadcast_to(bt, bc.shape)
    mc_ref[...] = jnp.broadcast_to(mch, bc.shape)


def _gates(a, b, i, f, alog, dtb, ib, fb):
    n = a.shape[0]
    return pl.pallas_call(
        _gates_kernel,
        out_shape=[jax.ShapeDtypeStruct((n, CHUNK), F32)] * 10,
        compiler_params=pltpu.CompilerParams(vmem_limit_bytes=VMEM_LIMIT),
        name="gates",
    )(a, b, i, f, alog, dtb, ib, fb)


def _gdn_kernel(qc_ref, kc_ref, vc_ref, zc_ref, ql_ref, kl_ref, vl_ref, zl_ref,
                cwq_ref, cwk_ref, cwv_ref, colg_ref, rowc_ref, rowsc_ref, egt_ref, gn_ref,
                oc_ref, ol_ref, pad_scr, qn, kn, vn, us, ws, os_, *, tc, tl):
    dk = HEAD_DIM
    blk = SUPER

    def conv(src_ref, w_ref, dst, off, t, norm):
        pad_scr[0:8, :] = jnp.zeros((8, dk), F32)
        pad_scr[8:8 + t, :] = src_ref[...]
        pad_scr[8 + t:16 + t, :] = jnp.zeros((8, dk), F32)
        w0, w1, w2 = w_ref[0:1, :], w_ref[1:2, :], w_ref[2:3, :]
        for r0 in range(0, t, blk):
            y = (w0 * pad_scr[r0 + 7:r0 + 7 + blk, :] + w1 * pad_scr[r0 + 8:r0 + 8 + blk, :]
                 + w2 * pad_scr[r0 + 9:r0 + 9 + blk, :])
            y = _silu(y)
            if norm is not None:
                y = y * lax.rsqrt(jnp.sum(y * y, axis=-1, keepdims=True) + EPS) * norm
            dst[off + r0:off + r0 + blk, :] = y

    for src_c, src_l, w_ref, dst, norm in ((qc_ref, ql_ref, cwq_ref, qn, dk ** -0.5),
                                           (kc_ref, kl_ref, cwk_ref, kn, 1.0),
                                           (vc_ref, vl_ref, cwv_ref, vn, None)):
        conv(src_c, w_ref, dst, 0, tc, norm)
        conv(src_l, w_ref, dst, tc, tl, norm)

    n_sc = (tc + tl) // blk
    ncc, ncl = tc // CHUNK, tl // CHUNK
    gn = gn_ref[...]

    for d in range(2):
        ri = lax.broadcasted_iota(jnp.int32, (blk, blk), 0)
        ci = lax.broadcasted_iota(jnp.int32, (blk, blk), 1)
        same = (ri // CHUNK) == (ci // CHUNK)
        tri_b = same & ((ri >= ci) if d == 0 else (ri <= ci))
        strict_b = same & ((ri > ci) if d == 0 else (ri < ci))
        eye_b = (ri == ci).astype(F32)

        def solve(sc, _):
            r0 = pl.multiple_of(sc * blk, blk)
            k = kn[pl.ds(r0, blk), :]
            v = vn[pl.ds(r0, blk), :]
            cg = colg_ref[d, pl.ds(r0, blk), :]
            beta, cum, ecum = cg[:, 0:1], cg[:, 1:2], cg[:, 2:3]
            crow = rowsc_ref[d, sc]
            dec = jnp.where(tri_b, jnp.exp(jnp.where(tri_b, cum - crow, 0.0)), 0.0)
            kb = k * beta
            m = jnp.where(strict_b, _bdot(kb, k, ((1,), (1,))) * dec, 0.0)
            p = eye_b - m
            mp = m
            for _ in range(5):
                mp = _dot3(mp, mp)
                p = p + _dot3(p, mp)
            us[pl.ds(r0, blk), :] = _bdot(p, v * beta)
            ws[pl.ds(r0, blk), :] = _bdot(p, kb * ecum)
            return 0

        lax.fori_loop(0, n_sc, solve, 0)

        r64 = lax.broadcasted_iota(jnp.int32, (CHUNK, CHUNK), 0)
        c64 = lax.broadcasted_iota(jnp.int32, (CHUNK, CHUNK), 1)
        tri = (r64 >= c64) if d == 0 else (r64 <= c64)

        def make_step(n, off_c, z_ref, o_ref):
            def step(i, state):
                cs = i if d == 0 else n - 1 - i
                c = off_c + cs
                r0 = pl.multiple_of(c * CHUNK, CHUNK)
                rs = pl.multiple_of(cs * CHUNK, CHUNK)
                q = qn[pl.ds(r0, CHUNK), :]
                k = kn[pl.ds(r0, CHUNK), :]
                cg = colg_ref[d, pl.ds(r0, CHUNK), :]
                cum, ecum, edk = cg[:, 1:2], cg[:, 2:3], cg[:, 3:4]
                crow = rowc_ref[d, c]
                dec = jnp.where(tri, jnp.exp(jnp.where(tri, cum - crow, 0.0)), 0.0)
                qk = jnp.where(tri, _bdot(q, k, ((1,), (1,))) * dec, 0.0)
                v_new = us[pl.ds(r0, CHUNK), :] - _bdot(ws[pl.ds(r0, CHUNK), :], state)
                o = _bdot(q * ecum, state) + _bdot(qk, v_new)
                state = state * egt_ref[d, c] + _bdot(k * edk, v_new, ((0,), (0,)))
                if d == 0:
                    os_[pl.ds(r0, CHUNK), :] = o
                else:
                    o = o + os_[pl.ds(r0, CHUNK), :]
                    z = z_ref[pl.ds(rs, CHUNK), :]
                    o_ref[pl.ds(rs, CHUNK), :] = (_rms(o, gn) * _silu(z)).astype(o_ref.dtype)
                return state
            return step

        s0 = jnp.zeros((dk, dk), F32)
        s1 = lax.fori_loop(0, ncc, make_step(ncc, 0, zc_ref, oc_ref), s0)
        lax.fori_loop(0, ncl, make_step(ncl, ncc, zl_ref, ol_ref), s1)


def _gdn(p_ctx, p_lat, B, conv_w, colg, rowc, rowsc, egt, gnorm):
    H = colg.shape[1]
    tc, tl = p_ctx.shape[0] // B, p_lat.shape[0] // B
    tt = tc + tl
    q0, k0, v0, z0 = 4, 4 + H, 4 + 2 * H, 4 + 3 * H

    def col(t, c0):
        return pl.BlockSpec((t, HEAD_DIM), lambda b, h: (b, c0 + h))

    def gate(*s):
        return pl.BlockSpec((2, None, None) + s, lambda b, h: (0, h, b) + (0,) * len(s))

    return pl.pallas_call(
        functools.partial(_gdn_kernel, tc=tc, tl=tl),
        grid=(B, H),
        in_specs=[col(tc, q0), col(tc, k0), col(tc, v0), col(tc, z0),
                  col(tl, q0), col(tl, k0), col(tl, v0), col(tl, z0),
                  pl.BlockSpec((3, HEAD_DIM), lambda b, h: (0, h)),
                  pl.BlockSpec((3, HEAD_DIM), lambda b, h: (0, H + h)),
                  pl.BlockSpec((3, HEAD_DIM), lambda b, h: (0, 2 * H + h)),
                  gate(tt, 4), gate(tt // CHUNK, 1, CHUNK), gate(tt // SUPER, 1, SUPER),
                  gate(tt // CHUNK, 1, HEAD_DIM),
                  pl.BlockSpec((1, HEAD_DIM), lambda b, h: (0, 0))],
        out_specs=[pl.BlockSpec((tc, HEAD_DIM), lambda b, h: (b, h)),
                   pl.BlockSpec((tl, HEAD_DIM), lambda b, h: (b, h))],
        out_shape=[jax.ShapeDtypeStruct((B * tc, H * HEAD_DIM), BF16),
                   jax.ShapeDtypeStruct((B * tl, H * HEAD_DIM), BF16)],
        scratch_shapes=[pltpu.VMEM((max(tc, tl) + 16, HEAD_DIM), F32)]
                       + [pltpu.VMEM((tt, HEAD_DIM), F32)] * 6,
        compiler_params=_cparams(("parallel", "parallel")),
        name="gdn",
    )(p_ctx, p_ctx, p_ctx, p_ctx, p_lat, p_lat, p_lat, p_lat, conv_w, conv_w, conv_w,
      colg, rowc, rowsc, egt, gnorm)


def _mlstm_kernel(qkc_ref, vc_ref, oc_ref, qkl_ref, vl_ref, ol_ref, colm_ref, rowm_ref,
                  scal_ref, gn_ref, hc_ref, hl_ref, hs, *, tc, tl):
    ncc, ncl = tc // CHUNK, tl // CHUNK
    gn = gn_ref[...]
    r64 = lax.broadcasted_iota(jnp.int32, (CHUNK, CHUNK), 0)
    c64 = lax.broadcasted_iota(jnp.int32, (CHUNK, CHUNK), 1)

    for d in range(2):
        tri = (r64 >= c64) if d == 0 else (r64 <= c64)

        def make_step(n, off_c, qk_ref, v_ref, og_ref, h_ref):
            def step(i, carry):
                c_st, n_st, m_st = carry
                cs = i if d == 0 else n - 1 - i
                c = off_c + cs
                r0 = pl.multiple_of(c * CHUNK, CHUNK)
                rs = pl.multiple_of(cs * CHUNK, CHUNK)
                qk = qk_ref[pl.ds(rs, CHUNK), :]
                q = qk[:, :ML_DK] * (ML_DK ** -0.5)
                k = qk[:, ML_DK:]
                v = v_ref[pl.ds(rs, CHUNK), :]
                cm = colm_ref[d, pl.ds(r0, CHUNK), :]
                b_col, ew = cm[:, 0:1], cm[:, 1:2]
                sc = scal_ref[d, c]
                bt, mch = sc[:, 0:1], sc[:, 1:2]
                log_d = jnp.where(tri, b_col + rowm_ref[d, c], -jnp.inf)
                m_intra = jnp.max(log_d, axis=-1, keepdims=True)
                s = _bdot(q, k, ((1,), (1,))) * jnp.exp(log_d - m_intra)
                num_intra = _bdot(s, v)
                den_intra = jnp.sum(s, axis=-1, keepdims=True)
                wk = k * ew
                kv_chunk = _bdot(wk, v, ((0,), (0,)))
                k_chunk = jnp.sum(wk, axis=0, keepdims=True)
                m_t = jnp.maximum(b_col + m_st, m_intra)
                inter = jnp.exp(b_col + m_st - m_t)
                intra = jnp.exp(m_intra - m_t)
                num = inter * _bdot(q, c_st) + intra * num_intra
                den = inter * jnp.sum(q * n_st, axis=-1, keepdims=True) + intra * den_intra
                h = num / jnp.maximum(jnp.abs(den), jnp.exp(-m_t))
                m_new = jnp.maximum(bt + m_st, mch)
                a_old = jnp.exp(bt + m_st - m_new)
                a_new = jnp.exp(mch - m_new)
                c_st = a_old * c_st + a_new * kv_chunk
                n_st = a_old * n_st + a_new * k_chunk
                if d == 0:
                    hs[pl.ds(r0, CHUNK), :] = h
                else:
                    h = h + hs[pl.ds(r0, CHUNK), :]
                    og = og_ref[pl.ds(rs, CHUNK), :]
                    h_ref[pl.ds(rs, CHUNK), :] = (_rms(h, gn) * _sigmoid(og)).astype(h_ref.dtype)
                return c_st, n_st, m_new
            return step

        carry = (jnp.zeros((ML_DK, HEAD_DIM), F32), jnp.zeros((1, ML_DK), F32),
                 jnp.zeros((1, 1), F32))
        carry = lax.fori_loop(0, ncc, make_step(ncc, 0, qkc_ref, vc_ref, oc_ref, hc_ref), carry)
        lax.fori_loop(0, ncl, make_step(ncl, ncc, qkl_ref, vl_ref, ol_ref, hl_ref), carry)


def _mlstm(p_ctx, p_ml, B, colm, rowm, scal, gnorm, qk0):
    H = colm.shape[1]
    tc, tl = p_ctx.shape[0] // B, p_ml.shape[0] // B
    tt = tc + tl

    def col(t, c0):
        return pl.BlockSpec((t, HEAD_DIM), lambda b, h: (b, c0 + h))

    def gate(*s):
        return pl.BlockSpec((2, None, None) + s, lambda b, h: (0, h, b) + (0,) * len(s))

    return pl.pallas_call(
        functools.partial(_mlstm_kernel, tc=tc, tl=tl),
        grid=(B, H),
        in_specs=[col(tc, qk0), col(tc, qk0 + H), col(tc, qk0 + 2 * H),
                  col(tl, 0), col(tl, H), col(tl, 2 * H),
                  gate(tt, 2), gate(tt // CHUNK, 1, CHUNK), gate(tt // CHUNK, 1, HEAD_DIM),
                  pl.BlockSpec((1, HEAD_DIM), lambda b, h: (0, 0))],
        out_specs=[pl.BlockSpec((tc, HEAD_DIM), lambda b, h: (b, h)),
                   pl.BlockSpec((tl, HEAD_DIM), lambda b, h: (b, h))],
        out_shape=[jax.ShapeDtypeStruct((B * tc, H * HEAD_DIM), BF16),
                   jax.ShapeDtypeStruct((B * tl, H * HEAD_DIM), BF16)],
        scratch_shapes=[pltpu.VMEM((tt, HEAD_DIM), F32)],
        compiler_params=_cparams(("parallel", "parallel")),
        name="mlstm",
    )(p_ctx, p_ctx, p_ctx, p_ml, p_ml, p_ml, colm, rowm, scal, gnorm)


def _in_sizes(D):
    s5 = D // 4
    gw = 3 * D // 8
    gh = gw // HEAD_DIM
    mw = D - s5 - gw
    mh = mw // HEAD_DIM
    return s5, gw, gh, mw, mh


def _permute_w_in(w, D):
    s5, gw, gh, mw, mh = _in_sizes(D)
    o_a = s5 + 4 * gw
    o_mq = o_a + 4 * gh
    o_mk = o_mq + mh * ML_DK
    o_mv = o_mk + mh * ML_DK
    o_mi = o_mv + 2 * mw
    end = o_mi + 4 * mh
    pieces = [w[:, :o_a]]
    for h in range(mh):
        pieces += [w[:, o_mq + h * ML_DK:o_mq + (h + 1) * ML_DK],
                   w[:, o_mk + h * ML_DK:o_mk + (h + 1) * ML_DK]]
    pieces += [w[:, o_mv:o_mi], w[:, o_a:o_mq], w[:, o_mi:end]]
    used = o_a + 2 * mh * ML_DK + 2 * mw + 4 * gh + 4 * mh
    total = -(-used // 512) * 512
    pieces.append(jnp.zeros((w.shape[0], total - used), w.dtype))
    gate0 = o_a + 2 * mh * ML_DK + 2 * mw
    return jnp.concatenate(pieces, axis=1).astype(BF16), gate0


def _token_mixer(p_ctx, p_lat, B, gate0, prm):
    (s5_lam_re, s5_lam_im, s5_log_dt, s5_b_re, s5_b_im, s5_c_re, s5_c_im, s5_d, s5_glu_w,
     s5_glu_b, gdn_conv_w, gdn_a_log, gdn_dt_bias, gdn_norm, ml_i_bias, ml_f_bias, ml_norm) = prm
    tc, tl = p_ctx.shape[0] // B, p_lat.shape[0] // B
    rows = tl // GRID_W
    H = gdn_a_log.shape[-1]
    nch = (tc + tl) // CHUNK

    s5_c, s5_l = _s5_mixer(p_ctx, p_lat, B, s5_lam_re, s5_lam_im, s5_log_dt, s5_b_re, s5_b_im,
                           s5_c_re, s5_c_im, s5_d, s5_glu_w, s5_glu_b)

    def col_major(a):
        return a.reshape(B, rows, GRID_W, -1).swapaxes(1, 2).reshape(B * tl, -1)

    def from_col_major(a):
        return a.reshape(B, GRID_W, rows, -1).swapaxes(1, 2).reshape(B * tl, -1)

    def gate_rows(kind, colmajor):
        c0 = gate0 + kind * 2 * H
        gc = p_ctx[:, c0:c0 + 2 * H]
        gl = p_lat[:, c0:c0 + 2 * H]
        if colmajor:
            gl = col_major(gl)
        g = jnp.concatenate([gc.reshape(B, tc // CHUNK, CHUNK, 2, H),
                             gl.reshape(B, tl // CHUNK, CHUNK, 2, H)], axis=1)
        return g.transpose(3, 4, 0, 1, 2).reshape(2 * H * B * nch, CHUNK)

    def par_rows(p):
        return jnp.broadcast_to(p[:, :, None, None], (2, H, B, nch)).reshape(-1, 1)

    (beta, cum, ecum, edk, egt, bc, r, ew, bt, mc) = _gates(
        gate_rows(0, False), gate_rows(1, False), gate_rows(2, True), gate_rows(3, True),
        par_rows(gdn_a_log), par_rows(gdn_dt_bias), par_rows(ml_i_bias), par_rows(ml_f_bias))

    lead = (2, H, B)
    colg = jnp.stack([beta, cum, ecum, edk], -1).reshape(lead + (nch * CHUNK, 4))
    rowc = cum.reshape(lead + (nch, 1, CHUNK))
    rowsc = cum.reshape(lead + (nch * CHUNK // SUPER, 1, SUPER))
    egtr = jnp.concatenate([egt, egt], -1).reshape(lead + (nch, 1, HEAD_DIM))
    gd_c, gd_l = _gdn(p_ctx, p_lat, B, gdn_conv_w, colg, rowc, rowsc, egtr,
                      gdn_norm.reshape(1, HEAD_DIM))

    colm = jnp.stack([bc, ew], -1).reshape(lead + (nch * CHUNK, 2))
    rowm = r.reshape(lead + (nch, 1, CHUNK))
    scal = jnp.concatenate([bt[:, :1], mc[:, :1], jnp.zeros((bt.shape[0], HEAD_DIM - 2), F32)],
                           -1).reshape(lead + (nch, 1, HEAD_DIM))
    qk0 = gate0 // HEAD_DIM - 3 * H
    p_ml = col_major(p_lat[:, qk0 * HEAD_DIM:gate0])
    ml_c, ml_l = _mlstm(p_ctx, p_ml, B, colm, rowm, scal, ml_norm.reshape(1, HEAD_DIM), qk0)
    ml_l = from_col_major(ml_l)
    return (s5_c, gd_c, ml_c), (s5_l, gd_l, ml_l)


def kernel(x, c, ctx, c_ctx, ada_w, ada_b, norm_mix_pre, norm_mix_post, norm_ffn_pre, norm_ffn_post, w_in, w_out, s5_lam_re, s5_lam_im, s5_log_dt, s5_b_re, s5_b_im, s5_c_re, s5_c_im, s5_d, s5_glu_w, s5_glu_b, gdn_conv_w, gdn_a_log, gdn_dt_bias, gdn_norm, mlstm_i_bias, mlstm_f_bias, mlstm_norm, ffn_w_gate, ffn_w_up, ffn_w_down):
    B, T, D = x.shape
    TC = ctx.shape[1]
    L = ada_w.shape[0]
    assert T % SUPER == 0 and TC % SUPER == 0 and T % GRID_W == 0 and B <= SUBLANES - 1
    s5w, gw, _, _, _ = _in_sizes(D)

    x_lat = x.reshape(B * T, D)
    x_ctx = ctx.reshape(B * TC, D)
    c8 = jnp.concatenate([c, c_ctx[None], jnp.zeros((SUBLANES - B - 1, D), F32)], 0)
    mods = _ada(c8, ada_w, ada_b).reshape(L * SUBLANES, 1, 6 * D)

    tm_lat = 1024
    tm_ctx = min(1024, B * TC)
    tiles_per_batch = T // tm_lat

    for l in range(L):
        lat_row = lambda i, l=l: l * SUBLANES + i // tiles_per_batch
        ctx_row = lambda i, l=l: l * SUBLANES + B
        g_mix_pre = norm_mix_pre[l].reshape(1, D)
        g_mix_post = norm_mix_post[l].reshape(1, D)
        g_ffn_pre = norm_ffn_pre[l].reshape(1, D)
        g_ffn_post = norm_ffn_post[l].reshape(1, D)

        w_in_p, gate0 = _permute_w_in(w_in[l], D)
        p_lat = _inproj(x_lat, mods, lat_row, g_mix_pre, w_in_p, tm_lat, 512)
        p_ctx = _inproj(x_ctx, mods, ctx_row, g_mix_pre, w_in_p, tm_ctx, 512)

        prm = (s5_lam_re[l], s5_lam_im[l], s5_log_dt[l], s5_b_re[l], s5_b_im[l], s5_c_re[l],
               s5_c_im[l], s5_d[l], s5_glu_w[l], s5_glu_b[l], gdn_conv_w[l], gdn_a_log[l],
               gdn_dt_bias[l], gdn_norm[l], mlstm_i_bias[l], mlstm_f_bias[l], mlstm_norm[l])
        mix_ctx, mix_lat = _token_mixer(p_ctx, p_lat, B, gate0, prm)

        wo = w_out[l].astype(BF16)
        wa, wb, wc = wo[:s5w], wo[s5w:s5w + gw], wo[s5w + gw:]
        wg = ffn_w_gate[l].astype(BF16)
        wu = ffn_w_up[l].astype(BF16)
        wd = ffn_w_down[l].astype(BF16)

        lat_row_o = lambda i, l=l: l * SUBLANES + i // (T // 256)
        lat_row_f = lambda i, l=l: l * SUBLANES + i // (T // 512)
        xs = _outproj(x_lat, mix_lat[0], mix_lat[1], mix_lat[2], mods, lat_row_o, g_mix_post,
                      wa, wb, wc, 256)
        x_lat = _ffn(xs, mods, lat_row_f, g_ffn_pre, g_ffn_post, wg, wu, wd, 512, 512)
        if l < L - 1:
            xs = _outproj(x_ctx, mix_ctx[0], mix_ctx[1], mix_ctx[2], mods, ctx_row, g_mix_post,
                          wa, wb, wc, 256)
            x_ctx = _ffn(xs, mods, ctx_row, g_ffn_pre, g_ffn_post, wg, wu, wd, 512, 512)
    return x_lat.reshape(B, T, D)
```

```python
import functools
import math

import jax
import jax.numpy as jnp
from jax import lax
from jax.experimental import pallas as pl
from jax.experimental.pallas import tpu as pltpu

F32 = jnp.float32
BF16 = jnp.bfloat16

EPS = 1e-6
GRID_W = 64
CHUNK = 64
SUPER = 4 * CHUNK
HEAD_DIM = 128
S5_GROUP = 16
S5_STATE = 64
S5_CHUNK = 16
S5_ROW = S5_CHUNK * S5_GROUP
SUBLANES = 8
ML_DK = 64

VMEM_LIMIT = 48 * 1024 * 1024


def _cparams(sem):
    return pltpu.CompilerParams(dimension_semantics=sem, vmem_limit_bytes=VMEM_LIMIT)


def _bdot(a, b, dims=((1,), (0,))):
    return lax.dot_general(a.astype(BF16), b.astype(BF16), (dims, ((), ())),
                           preferred_element_type=F32)


def _split3(a):
    a0 = a.astype(BF16)
    r1 = a - a0.astype(F32)
    a1 = r1.astype(BF16)
    a2 = (r1 - a1.astype(F32)).astype(BF16)
    return a0, a1, a2


def _dot3(a, b):
    a0, a1, _ = _split3(a)
    b0, b1, _ = _split3(b)
    d = functools.partial(jnp.dot, preferred_element_type=F32)
    return d(a0, b0) + (d(a0, b1) + d(a1, b0))


def _dot_exact_rhs(a, b01):
    a0, a1, a2 = _split3(a)
    b = b01.astype(BF16)
    d = functools.partial(jnp.dot, preferred_element_type=F32)
    return d(a0, b) + (d(a1, b) + d(a2, b))


def _sigmoid(x):
    return 1.0 / (1.0 + jnp.exp(-x))


def _silu(x):
    return x * _sigmoid(x)


def _softplus(x):
    return jnp.maximum(x, 0.0) + jnp.log(1.0 + jnp.exp(-jnp.abs(x)))


def _rms(x, g):
    return x * lax.rsqrt(jnp.mean(x * x, axis=-1, keepdims=True) + EPS) * g


def _ada_kernel(c_ref, w_ref, b_ref, o_ref):
    c = c_ref[...]
    o_ref[...] = _bdot(_silu(c), w_ref[...]) + b_ref[...]


def _ada(c8, ada_w, ada_b):
    L, D, N = ada_w.shape
    tn = 1024
    return pl.pallas_call(
        _ada_kernel,
        grid=(L, N // tn),
        in_specs=[pl.BlockSpec((SUBLANES, D), lambda l, j: (0, 0)),
                  pl.BlockSpec((None, D, tn), lambda l, j: (l, 0, j)),
                  pl.BlockSpec((None, 1, tn), lambda l, j: (l, 0, j))],
        out_specs=pl.BlockSpec((None, SUBLANES, tn), lambda l, j: (l, 0, j)),
        out_shape=jax.ShapeDtypeStruct((L, SUBLANES, N), F32),
        compiler_params=_cparams(("parallel", "arbitrary")),
        name="ada",
    )(c8, ada_w, ada_b.reshape(L, 1, N))


def _inproj_kernel(x_ref, sh_ref, sc_ref, g_ref, w_ref, o_ref, h_scr):
    @pl.when(pl.program_id(1) == 0)
    def _():
        h = _rms(x_ref[...], g_ref[...]) * (1.0 + sc_ref[0]) + sh_ref[0]
        h_scr[...] = h.astype(BF16)

    o_ref[...] = jnp.dot(h_scr[...], w_ref[...], preferred_element_type=F32)


def _inproj(x, mods, mod_row, g_pre, w, tm, tn):
    M, D = x.shape
    N = w.shape[1]
    return pl.pallas_call(
        _inproj_kernel,
        grid=(M // tm, N // tn),
        in_specs=[pl.BlockSpec((tm, D), lambda i, j: (i, 0)),
                  pl.BlockSpec((1, 1, D), lambda i, j: (mod_row(i), 0, 0)),
                  pl.BlockSpec((1, 1, D), lambda i, j: (mod_row(i), 0, 1)),
                  pl.BlockSpec((1, D), lambda i, j: (0, 0)),
                  pl.BlockSpec((D, tn), lambda i, j: (0, j))],
        out_specs=pl.BlockSpec((tm, tn), lambda i, j: (i, j)),
        out_shape=jax.ShapeDtypeStruct((M, N), F32),
        scratch_shapes=[pltpu.VMEM((tm, D), BF16)],
        compiler_params=_cparams(("parallel", "arbitrary")),
        name="inproj",
    )(x, mods, mods, g_pre, w)


def _outproj_kernel(x_ref, a_ref, b_ref, c_ref, gt_ref, g_ref, wa_ref, wb_ref, wc_ref, o_ref):
    acc = jnp.dot(a_ref[...], wa_ref[...], preferred_element_type=F32)
    acc += jnp.dot(b_ref[...], wb_ref[...], preferred_element_type=F32)
    acc += jnp.dot(c_ref[...], wc_ref[...], preferred_element_type=F32)
    o_ref[...] = x_ref[...] + gt_ref[0] * _rms(acc, g_ref[...])


def _outproj(x, ma, mb, mc, mods, mod_row, g_post, wa, wb, wc, tm):
    M, D = x.shape
    row = lambda i: (i, 0)
    full = lambda i: (0, 0)
    return pl.pallas_call(
        _outproj_kernel,
        grid=(M // tm,),
        in_specs=[pl.BlockSpec((tm, D), row),
                  pl.BlockSpec((tm, ma.shape[1]), row),
                  pl.BlockSpec((tm, mb.shape[1]), row),
                  pl.BlockSpec((tm, mc.shape[1]), row),
                  pl.BlockSpec((1, 1, D), lambda i: (mod_row(i), 0, 2)),
                  pl.BlockSpec((1, D), full),
                  pl.BlockSpec(wa.shape, full),
                  pl.BlockSpec(wb.shape, full),
                  pl.BlockSpec(wc.shape, full)],
        out_specs=pl.BlockSpec((tm, D), row),
        out_shape=jax.ShapeDtypeStruct((M, D), F32),
        compiler_params=_cparams(("parallel",)),
        name="outproj",
    )(x, ma, mb, mc, mods, g_post, wa, wb, wc)


def _ffn_kernel(x_ref, sh_ref, sc_ref, gt_ref, gpre_ref, gpost_ref, wg_ref, wu_ref, wd_ref,
                o_ref, h_scr, acc_scr):
    j = pl.program_id(1)

    @pl.when(j == 0)
    def _():
        h = _rms(x_ref[...], gpre_ref[...]) * (1.0 + sc_ref[0]) + sh_ref[0]
        h_scr[...] = h.astype(BF16)
        acc_scr[...] = jnp.zeros_like(acc_scr)

    h = h_scr[...]
    g = jnp.dot(h, wg_ref[...], preferred_element_type=F32)
    u = jnp.dot(h, wu_ref[...], preferred_element_type=F32)
    a = (_silu(g) * u).astype(BF16)
    acc_scr[...] += jnp.dot(a, wd_ref[...], preferred_element_type=F32)

    @pl.when(j == pl.num_programs(1) - 1)
    def _():
        o_ref[...] = x_ref[...] + gt_ref[0] * _rms(acc_scr[...], gpost_ref[...])


def _ffn(x, mods, mod_row, g_pre, g_post, wg, wu, wd, tm, th):
    M, D = x.shape
    H = wg.shape[1]
    row = lambda i, j: (i, 0)
    full = lambda i, j: (0, 0)
    return pl.pallas_call(
        _ffn_kernel,
        grid=(M // tm, H // th),
        in_specs=[pl.BlockSpec((tm, D), row),
                  pl.BlockSpec((1, 1, D), lambda i, j: (mod_row(i), 0, 3)),
                  pl.BlockSpec((1, 1, D), lambda i, j: (mod_row(i), 0, 4)),
                  pl.BlockSpec((1, 1, D), lambda i, j: (mod_row(i), 0, 5)),
                  pl.BlockSpec((1, D), full),
                  pl.BlockSpec((1, D), full),
                  pl.BlockSpec((D, th), lambda i, j: (0, j)),
                  pl.BlockSpec((D, th), lambda i, j: (0, j)),
                  pl.BlockSpec((th, D), lambda i, j: (j, 0))],
        out_specs=pl.BlockSpec((tm, D), row),
        out_shape=jax.ShapeDtypeStruct((M, D), F32),
        scratch_shapes=[pltpu.VMEM((tm, D), BF16), pltpu.VMEM((tm, D), F32)],
        compiler_params=_cparams(("parallel", "arbitrary")),
        name="ffn",
    )(x, mods, mods, mods, g_pre, g_post, wg, wu, wd)


def _s5_param_kernel(lre_ref, lim_ref, ldt_ref, bre_ref, bim_ref, cre_ref, cim_ref,
                     k_ref, care_ref, ncaim_ref, abre_ref, abim_ref, alre_ref, alim_ref):
    dt = jnp.exp(ldt_ref[...])
    lre, lim = lre_ref[...], lim_ref[...]
    mag = jnp.exp(lre * dt)
    ar = mag * jnp.cos(lim * dt)
    ai = mag * jnp.sin(lim * dt)
    den = lre * lre + lim * lim
    f_re = ((ar - 1.0) * lre + ai * lim) / den
    f_im = (ai * lre - (ar - 1.0) * lim) / den
    b_re, b_im = bre_ref[...], bim_ref[...]
    bb_re = f_re * b_re - f_im * b_im
    bb_im = f_re * b_im + f_im * b_re
    c_re, c_im = cre_ref[...], cim_ref[...]

    def lag_kernel(ca, bb):
        return jnp.einsum('ghp,gjp->ghj', ca, bb, precision=lax.Precision.HIGHEST,
                          preferred_element_type=F32)

    pr, pi = jnp.ones_like(ar), jnp.zeros_like(ar)
    for k in range(S5_CHUNK + 1):
        ca_re = c_re * pr - c_im * pi
        ca_im = c_re * pi + c_im * pr
        care_ref[k] = ca_re
        ncaim_ref[k] = -ca_im
        if k < S5_CHUNK:
            k_ref[k] = lag_kernel(ca_re, bb_re) - lag_kernel(ca_im, bb_im)
            abre_ref[k] = pr * bb_re - pi * bb_im
            abim_ref[k] = pr * bb_im + pi * bb_re
        else:
            alre_ref[...] = pr
            alim_ref[...] = pi
        pr, pi = pr * ar - pi * ai, pr * ai + pi * ar


def _s5_params(lam_re, lam_im, log_dt, bt_re, bt_im, c_re, c_im):
    _, G, P = lam_re.shape
    Hg = bt_re.shape[-2]
    Lc = S5_CHUNK
    gb = 8
    par = lambda *s: pl.BlockSpec((None, gb) + s, lambda d, g: (d, g) + (0,) * len(s))
    lag = lambda n, *s: pl.BlockSpec((None, n, gb) + s, lambda d, g: (d, 0, g) + (0,) * len(s))
    return pl.pallas_call(
        _s5_param_kernel,
        grid=(2, G // gb),
        in_specs=[par(1, P), par(1, P), par(1, 1), par(Hg, P), par(Hg, P), par(Hg, P), par(Hg, P)],
        out_specs=[lag(Lc, Hg, Hg), lag(Lc + 1, Hg, P), lag(Lc + 1, Hg, P),
                   lag(Lc, Hg, P), lag(Lc, Hg, P), par(1, P), par(1, P)],
        out_shape=[jax.ShapeDtypeStruct((2, Lc, G, Hg, Hg), F32),
                   jax.ShapeDtypeStruct((2, Lc + 1, G, Hg, P), F32),
                   jax.ShapeDtypeStruct((2, Lc + 1, G, Hg, P), F32),
                   jax.ShapeDtypeStruct((2, Lc, G, Hg, P), F32),
                   jax.ShapeDtypeStruct((2, Lc, G, Hg, P), F32),
                   jax.ShapeDtypeStruct((2, G, 1, P), F32),
                   jax.ShapeDtypeStruct((2, G, 1, P), F32)],
        compiler_params=_cparams(("parallel", "parallel")),
        name="s5_params",
    )(lam_re.reshape(2, G, 1, P), lam_im.reshape(2, G, 1, P), log_dt.reshape(2, G, 1, 1),
      bt_re, bt_im, c_re, c_im)


def _s5_scan_kernel(u_ref, tt_ref, ore_ref, oim_ref, wre_ref, wim_ref, al_ref, y_ref,
                    sin_re, sin_im, sst_re, sst_im, *, nc_ctx, nc_lat):
    u = u_ref[...].astype(BF16)
    nb = SUBLANES
    y = None
    for d in range(2):
        sin_re[...] = jnp.dot(u, wre_ref[d], preferred_element_type=F32)
        sin_im[...] = jnp.dot(u, wim_ref[d], preferred_element_type=F32)
        ar = al_ref[2 * d:2 * d + 1, :]
        ai = al_ref[2 * d + 1:2 * d + 2, :]

        def make_body(base, n):
            def body(i, carry):
                sr, si = carry
                c = i if d == 0 else n - 1 - i
                row = pl.multiple_of(base + c * nb, nb)
                sst_re[pl.ds(row, nb), :] = sr
                sst_im[pl.ds(row, nb), :] = si
                xr = sin_re[pl.ds(row, nb), :]
                xi = sin_im[pl.ds(row, nb), :]
                return ar * sr - ai * si + xr, ar * si + ai * sr + xi
            return body

        z = jnp.zeros((nb, S5_STATE), F32)
        carry = lax.fori_loop(0, nc_ctx, make_body(0, nc_ctx), (z, z))
        lax.fori_loop(0, nc_lat, make_body(nc_ctx * nb, nc_lat), carry)
        yd = jnp.dot(u, tt_ref[d], preferred_element_type=F32)
        yd += _bdot(sst_re[...], ore_ref[d])
        yd += _bdot(sst_im[...], oim_ref[d])
        y = yd if y is None else y + yd
    y_ref[...] = y


def _s5_scan(u, tt, ore, oim, wre, wim, al, nc_ctx, nc_lat):
    G, R, W = u.shape
    P = S5_STATE
    g3 = lambda *s: pl.BlockSpec((None,) + s, lambda g: (g,) + (0,) * len(s))
    return pl.pallas_call(
        functools.partial(_s5_scan_kernel, nc_ctx=nc_ctx, nc_lat=nc_lat),
        grid=(G,),
        in_specs=[g3(R, W), g3(2, W, W), g3(2, P, W), g3(2, P, W), g3(2, W, P), g3(2, W, P),
                  g3(4, P)],
        out_specs=g3(R, W),
        out_shape=jax.ShapeDtypeStruct((G, R, W), F32),
        scratch_shapes=[pltpu.VMEM((R, P), F32)] * 4,
        compiler_params=_cparams(("parallel",)),
        name="s5_scan",
    )(u, tt, ore, oim, wre, wim, al)


def _s5_out_kernel(y_ref, u_ref, d_ref, w_ref, b_ref, o_ref):
    x = y_ref[...] + d_ref[...] * u_ref[...]
    y = 0.5 * x * (1.0 + jnp.tanh(math.sqrt(2.0 / math.pi) * (x + 0.044715 * (x * x * x))))
    gate = _bdot(y, w_ref[...]) + b_ref[...]
    o_ref[...] = (y * _sigmoid(gate)).astype(o_ref.dtype)


def _s5_out(y, p, d, w, b, tm):
    M, W = y.shape
    row = lambda i: (i, 0)
    full = lambda i: (0, 0)
    return pl.pallas_call(
        _s5_out_kernel,
        grid=(M // tm,),
        in_specs=[pl.BlockSpec((tm, W), row), pl.BlockSpec((tm, W), row),
                  pl.BlockSpec((1, W), full), pl.BlockSpec((W, W), full),
                  pl.BlockSpec((1, W), full)],
        out_specs=pl.BlockSpec((tm, W), row),
        out_shape=jax.ShapeDtypeStruct((M, W), BF16),
        compiler_params=_cparams(("parallel",)),
        name="s5_out",
    )(y, p, d, w, b)


def _s5_mixer(p_ctx, p_lat, B, lam_re, lam_im, log_dt, b_re, b_im, c_re, c_im, d, glu_w, glu_b):
    G, Hg, P, Lc = lam_re.shape[1], S5_GROUP, S5_STATE, S5_CHUNK
    W = G * Hg
    kk, care, ncaim, abre, abim, alre, alim = _s5_params(
        lam_re, lam_im, log_dt, b_re.swapaxes(-1, -2), b_im.swapaxes(-1, -2), c_re, c_im)
    s_i = jnp.arange(Lc)[:, None]
    t_i = jnp.arange(Lc)[None, :]

    def toeplitz(k, lag):
        m = jnp.where((lag >= 0)[:, :, None, None, None], k[jnp.clip(lag, 0, Lc - 1)], 0.0)
        return m.transpose(2, 0, 4, 1, 3).reshape(G, S5_ROW, S5_ROW)

    tt = jnp.stack([toeplitz(kk[0], t_i - s_i), toeplitz(kk[1], s_i - t_i)], 1).astype(BF16)

    def readout(ca):
        f = ca[0, 1:]
        b = ca[1, :0:-1]
        o = jnp.stack([f, b], 0)
        return o.transpose(2, 0, 4, 1, 3).reshape(G, 2, P, S5_ROW).astype(BF16)

    def writein(ab):
        w = jnp.stack([ab[0, ::-1], ab[1]], 0)
        return w.transpose(2, 0, 1, 3, 4).reshape(G, 2, S5_ROW, P).astype(BF16)

    ore, oim = readout(care), readout(ncaim)
    wre, wim = writein(abre), writein(abim)
    al = jnp.stack([alre[0], alim[0], alre[1], alim[1]], 1).reshape(G, 4, P)

    def to_rows(p):
        nc = p.shape[0] // B // Lc
        u = p[:, :W].reshape(B, nc, Lc, G, Hg).transpose(3, 1, 0, 2, 4)
        u = jnp.pad(u, ((0, 0), (0, 0), (0, SUBLANES - B), (0, 0), (0, 0)))
        return u.reshape(G, nc * SUBLANES, S5_ROW), nc

    def from_rows(y, nc):
        y = y.reshape(G, nc, SUBLANES, Lc, Hg)[:, :, :B]
        return y.transpose(2, 1, 3, 0, 4).reshape(B * nc * Lc, W)

    u_c, nc_ctx = to_rows(p_ctx)
    u_l, nc_lat = to_rows(p_lat)
    y = _s5_scan(jnp.concatenate([u_c, u_l], 1), tt, ore, oim, wre, wim, al, nc_ctx, nc_lat)
    y_c = from_rows(y[:, :nc_ctx * SUBLANES], nc_ctx)
    y_l = from_rows(y[:, nc_ctx * SUBLANES:], nc_lat)
    dd = d.reshape(1, W)
    gw = glu_w.astype(BF16)
    gb = glu_b.reshape(1, W)
    return (_s5_out(y_c, p_ctx, dd, gw, gb, min(1024, y_c.shape[0])),
            _s5_out(y_l, p_lat, dd, gw, gb, min(1024, y_l.shape[0])))


def _gates_kernel(a_ref, b_ref, i_ref, f_ref, alog_ref, dtb_ref, ib_ref, fb_ref,
                  beta_ref, cum_ref, ecum_ref, edk_ref, egt_ref,
                  bc_ref, r_ref, ew_ref, bt_ref, mc_ref):
    n = a_ref.shape[0] // 2
    r_i = lax.broadcasted_iota(jnp.int32, (CHUNK, CHUNK), 0)
    c_i = lax.broadcasted_iota(jnp.int32, (CHUNK, CHUNK), 1)
    incl = [(r_i <= c_i).astype(F32), (r_i >= c_i).astype(F32)]

    def cumsum(x):
        return jnp.concatenate([_dot_exact_rhs(x[:n], incl[0]), _dot_exact_rhs(x[n:], incl[1])], 0)

    g = -jnp.exp(alog_ref[...]) * _softplus(a_ref[...] + dtb_ref[...])
    cum = cumsum(g)
    tot = jnp.sum(g, axis=-1, keepdims=True)
    beta_ref[...] = _sigmoid(b_ref[...])
    cum_ref[...] = cum
    ecum_ref[...] = jnp.exp(cum)
    edk_ref[...] = jnp.exp(tot - cum)
    egt_ref[...] = jnp.broadcast_to(jnp.exp(tot), cum.shape)

    ig = i_ref[...] + ib_ref[...]
    lf = -_softplus(-(f_ref[...] + fb_ref[...]))
    bc = cumsum(lf)
    bt = jnp.sum(lf, axis=-1, keepdims=True)
    log_w = bt - bc + ig
    mch = jnp.max(log_w, axis=-1, keepdims=True)
    bc_ref[...] = bc
    r_ref[...] = ig - bc
    ew_ref[...] = jnp.exp(log_w - mch)
    bt_ref[...] = jnp.broadcast_to(bt, bc.shape)
    mc_ref[...] = jnp.broadcast_to(mch, bc.shape)


def _gates(a, b, i, f, alog, dtb, ib, fb):
    n = a.shape[0]
    return pl.pallas_call(
        _gates_kernel,
        out_shape=[jax.ShapeDtypeStruct((n, CHUNK), F32)] * 10,
        compiler_params=pltpu.CompilerParams(vmem_limit_bytes=VMEM_LIMIT),
        name="gates",
    )(a, b, i, f, alog, dtb, ib, fb)


def _gdn_kernel(qc_ref, kc_ref, vc_ref, zc_ref, ql_ref, kl_ref, vl_ref, zl_ref,
                cwq_ref, cwk_ref, cwv_ref, colg_ref, rowsc_ref, egt_ref, gn_ref,
                oc_ref, ol_ref, pad_scr, qn, kn, vn, a_scr, b_scr, qp_scr, op_scr, o_scr,
                *, tc, tl):
    dk = HEAD_DIM
    blk = SUPER

    def conv(src_ref, w_ref, dst, off, t, norm):
        pad_scr[0:8, :] = jnp.zeros((8, dk), F32)
        pad_scr[8:8 + t, :] = src_ref[...]
        pad_scr[8 + t:16 + t, :] = jnp.zeros((8, dk), F32)
        w0, w1, w2 = w_ref[0:1, :], w_ref[1:2, :], w_ref[2:3, :]
        for r0 in range(0, t, blk):
            y = (w0 * pad_scr[r0 + 7:r0 + 7 + blk, :] + w1 * pad_scr[r0 + 8:r0 + 8 + blk, :]
                 + w2 * pad_scr[r0 + 9:r0 + 9 + blk, :])
            y = _silu(y)
            if norm is not None:
                y = y * lax.rsqrt(jnp.sum(y * y, axis=-1, keepdims=True) + EPS) * norm
            dst[off + r0:off + r0 + blk, :] = y

    for src_c, src_l, w_ref, dst, norm in ((qc_ref, ql_ref, cwq_ref, qn, dk ** -0.5),
                                           (kc_ref, kl_ref, cwk_ref, kn, 1.0),
                                           (vc_ref, vl_ref, cwv_ref, vn, None)):
        conv(src_c, w_ref, dst, 0, tc, norm)
        conv(src_l, w_ref, dst, tc, tl, norm)

    n_sc = (tc + tl) // blk
    ncc, ncl = tc // CHUNK, tl // CHUNK
    per = blk // CHUNK
    nt = ((1,), (1,))

    ri = lax.broadcasted_iota(jnp.int32, (blk, blk), 0)
    ci = lax.broadcasted_iota(jnp.int32, (blk, blk), 1)
    same = (ri // CHUNK) == (ci // CHUNK)
    eye_b = (ri == ci).astype(F32)
    tri_m = (same & (ri >= ci), same & (ri <= ci))
    strict_m = (same & (ri > ci), same & (ri < ci))
    lr = lax.broadcasted_iota(jnp.int32, (per * dk, blk), 0)
    lc = lax.broadcasted_iota(jnp.int32, (per * dk, blk), 1)
    chunk_sel = (lr // dk) == (lc // CHUNK)

    def prepare(sc, _):
        r0 = pl.multiple_of(sc * blk, blk)
        q = qn[pl.ds(r0, blk), :]
        k = kn[pl.ds(r0, blk), :]
        v = vn[pl.ds(r0, blk), :]
        qkt = _bdot(q, k, nt)
        dirs = (0, 1)
        cgs = [colg_ref[d, pl.ds(r0, blk), :] for d in dirs]
        decs = [jnp.where(tri_m[d], jnp.exp(jnp.where(tri_m[d], cgs[d][:, 1:2] - rowsc_ref[d, sc], 0.0)),
                          0.0) for d in dirs]
        kbs = [k * cgs[d][:, 0:1] for d in dirs]
        mps = [jnp.where(strict_m[d], _bdot(kbs[d], k, nt) * decs[d], 0.0) for d in dirs]
        ps = [eye_b - mp for mp in mps]
        for _ in range(5):
            mps = [_bdot(mp, mp) for mp in mps]
            ps = [p + _bdot(p, mp) for p, mp in zip(ps, mps)]
        uws = [_bdot(ps[d], jnp.concatenate([v * cgs[d][:, 0:1], kbs[d] * cgs[d][:, 2:3]], axis=1))
               for d in dirs]
        ows = [_bdot(jnp.where(tri_m[d], qkt * decs[d], 0.0), uws[d]) for d in dirs]
        lhss = [jnp.where(chunk_sel, jnp.concatenate([(k * cgs[d][:, 3:4]).T] * per, axis=0), 0.0)
                for d in dirs]
        abs_ = [_bdot(lhss[d], uws[d]) for d in dirs]
        ra = pl.multiple_of(sc * (per * dk), per * dk)
        for d in dirs:
            qp_scr[d, pl.ds(r0, blk), :] = (q * cgs[d][:, 2:3] - ows[d][:, dk:]).astype(BF16)
            op_scr[d, pl.ds(r0, blk), :] = ows[d][:, :dk]
            a_scr[d, pl.ds(ra, per * dk), :] = abs_[d][:, dk:].astype(BF16)
            b_scr[d, pl.ds(ra, per * dk), :] = abs_[d][:, :dk]
        return 0

    lax.fori_loop(0, n_sc, prepare, 0)

    def make_step(n, off_c):
        def step(i, carry):
            nxt = []
            for d in range(2):
                s = carry[d]
                c = off_c + (i if d == 0 else n - 1 - i)
                r0 = pl.multiple_of(c * CHUNK, CHUNK)
                ra = pl.multiple_of(c * dk, dk)
                sb = s.astype(BF16)
                o = jnp.dot(qp_scr[d, pl.ds(r0, CHUNK), :], sb, preferred_element_type=F32)
                o_scr[d, pl.ds(r0, CHUNK), :] = o + op_scr[d, pl.ds(r0, CHUNK), :]
                nxt.append(s * egt_ref[d, c] + b_scr[d, pl.ds(ra, dk), :]
                           - jnp.dot(a_scr[d, pl.ds(ra, dk), :], sb, preferred_element_type=F32))
            return tuple(nxt)
        return step

    s0 = jnp.zeros((dk, dk), F32)
    carry = lax.fori_loop(0, ncc, make_step(ncc, 0), (s0, s0), unroll=2)
    lax.fori_loop(0, ncl, make_step(ncl, ncc), carry, unroll=2)

    gn = gn_ref[...]
    for z_ref, o_ref, off, t in ((zc_ref, oc_ref, 0, tc), (zl_ref, ol_ref, tc, tl)):
        for r0 in range(0, t, blk):
            o = o_scr[0, off + r0:off + r0 + blk, :] + o_scr[1, off + r0:off + r0 + blk, :]
            o_ref[r0:r0 + blk, :] = (_rms(o, gn) * _silu(z_ref[r0:r0 + blk, :])).astype(o_ref.dtype)


def _gdn(p_ctx, p_lat, B, conv_w, colg, rowsc, egt, gnorm):
    H = colg.shape[1]
    tc, tl = p_ctx.shape[0] // B, p_lat.shape[0] // B
    tt = tc + tl
    q0, k0, v0, z0 = 4, 4 + H, 4 + 2 * H, 4 + 3 * H

    def col(t, c0):
        return pl.BlockSpec((t, HEAD_DIM), lambda b, h: (b, c0 + h))

    def gate(*s):
        return pl.BlockSpec((2, None, None) + s, lambda b, h: (0, h, b) + (0,) * len(s))

    return pl.pallas_call(
        functools.partial(_gdn_kernel, tc=tc, tl=tl),
        grid=(B, H),
        in_specs=[col(tc, q0), col(tc, k0), col(tc, v0), col(tc, z0),
                  col(tl, q0), col(tl, k0), col(tl, v0), col(tl, z0),
                  pl.BlockSpec((3, HEAD_DIM), lambda b, h: (0, h)),
                  pl.BlockSpec((3, HEAD_DIM), lambda b, h: (0, H + h)),
                  pl.BlockSpec((3, HEAD_DIM), lambda b, h: (0, 2 * H + h)),
                  gate(tt, 4), gate(tt // SUPER, 1, SUPER), gate(tt // CHUNK, 1, HEAD_DIM),
                  pl.BlockSpec((1, HEAD_DIM), lambda b, h: (0, 0))],
        out_specs=[pl.BlockSpec((tc, HEAD_DIM), lambda b, h: (b, h)),
                   pl.BlockSpec((tl, HEAD_DIM), lambda b, h: (b, h))],
        out_shape=[jax.ShapeDtypeStruct((B * tc, H * HEAD_DIM), BF16),
                   jax.ShapeDtypeStruct((B * tl, H * HEAD_DIM), BF16)],
        scratch_shapes=[pltpu.VMEM((max(tc, tl) + 16, HEAD_DIM), F32)]
                       + [pltpu.VMEM((tt, HEAD_DIM), F32)] * 3
                       + [pltpu.VMEM((2, tt // CHUNK * HEAD_DIM, HEAD_DIM), BF16),
                          pltpu.VMEM((2, tt // CHUNK * HEAD_DIM, HEAD_DIM), F32),
                          pltpu.VMEM((2, tt, HEAD_DIM), BF16),
                          pltpu.VMEM((2, tt, HEAD_DIM), F32),
                          pltpu.VMEM((2, tt, HEAD_DIM), F32)],
        compiler_params=_cparams(("parallel", "parallel")),
        name="gdn",
    )(p_ctx, p_ctx, p_ctx, p_ctx, p_lat, p_lat, p_lat, p_lat, conv_w, conv_w, conv_w,
      colg, rowsc, egt, gnorm)


def _mlstm_kernel(qkc_ref, vc_ref, oc_ref, qkl_ref, vl_ref, ol_ref, colm_ref, rowsm_ref,
                  scal_ref, gn_ref, hc_ref, hl_ref, na_scr, kva_scr, qs_scr, ms_scr, h_scr,
                  *, tc, tl):
    blk, dk, dv = SUPER, ML_DK, HEAD_DIM
    per = blk // CHUNK
    ncc, ncl = tc // CHUNK, tl // CHUNK
    qscale = dk ** -0.5

    ri = lax.broadcasted_iota(jnp.int32, (blk, blk), 0)
    ci = lax.broadcasted_iota(jnp.int32, (blk, blk), 1)
    same = (ri // CHUNK) == (ci // CHUNK)
    tri_m = (same & (ri >= ci), same & (ri <= ci))
    lane2 = lax.broadcasted_iota(jnp.int32, (blk, 2 * dv), 1)
    one_col = (lax.broadcasted_iota(jnp.int32, (blk, dv), 1) == 0).astype(F32)

    for d in range(2):
        def make_scalar_step(n, off_c):
            def step(i, m_st):
                c = off_c + (i if d == 0 else n - 1 - i)
                bt = scal_ref[d, c, 0:1, :]
                mch = scal_ref[d, c, 1:2, :]
                m_new = jnp.maximum(bt + m_st, mch)
                ms_scr[d, c, 0:1, :] = m_st
                ms_scr[d, c, 1:2, :] = jnp.exp(bt + m_st - m_new)
                ms_scr[d, c, 2:3, :] = jnp.exp(mch - m_new)
                return m_new
            return step
        m_c = lax.fori_loop(0, ncc, make_scalar_step(ncc, 0), jnp.zeros((1, 2 * dv), F32))
        lax.fori_loop(0, ncl, make_scalar_step(ncl, ncc), m_c)

    rblk = lax.broadcasted_iota(jnp.int32, (blk, 1), 0) // CHUNK

    def per_chunk_col(vals):
        col = vals[per - 1]
        for j in range(per - 2, -1, -1):
            col = jnp.where(rblk == j, vals[j], col)
        return col

    def make_prepare(qk_ref, v_ref, off_sc):
        def prepare(s, _):
            rs = pl.multiple_of(s * blk, blk)
            sc = off_sc + s
            r0 = pl.multiple_of(sc * blk, blk)
            qk = qk_ref[pl.ds(rs, blk), :]
            q = qk[:, :dk] * qscale
            k = qk[:, dk:]
            va = jnp.concatenate([v_ref[pl.ds(rs, blk), :], one_col], axis=1)
            qkt = _bdot(q, k, ((1,), (1,)))
            for d in range(2):
                cm = colm_ref[d, pl.ds(r0, blk), :]
                b_col, ew = cm[:, 0:1], cm[:, 1:2]
                ms = [ms_scr[d, sc * per + j] for j in range(per)]
                m_st = per_chunk_col([m[0:1, 0:1] for m in ms])
                a_new = per_chunk_col([m[2:3, 0:1] for m in ms])
                log_d = jnp.where(tri_m[d], b_col + rowsm_ref[d, sc], -jnp.inf)
                m_t = jnp.maximum(b_col + m_st, jnp.max(log_d, axis=-1, keepdims=True))
                na = _bdot(qkt * jnp.exp(log_d - m_t), va)
                na_scr[d, pl.ds(r0, blk), :] = jnp.where(lane2 == dv + 1, jnp.exp(-m_t), na)
                qs_scr[d, pl.ds(r0, blk), :] = (q * jnp.exp(b_col + m_st - m_t)).astype(BF16)
                wkt = (k * (ew * a_new)).T
                lhs = jnp.where(same, jnp.concatenate([wkt] * per, axis=0), 0.0)
                kva_scr[d, pl.ds(r0, blk), :] = _bdot(lhs, va)
            return 0
        return prepare

    lax.fori_loop(0, tc // blk, make_prepare(qkc_ref, vc_ref, 0), 0)
    lax.fori_loop(0, tl // blk, make_prepare(qkl_ref, vl_ref, tc // blk), 0)

    def make_step(n, off_c):
        def step(i, carry):
            nxt = []
            for d in range(2):
                c_st = carry[d]
                c = off_c + (i if d == 0 else n - 1 - i)
                r0 = pl.multiple_of(c * CHUNK, CHUNK)
                na = na_scr[d, pl.ds(r0, CHUNK), :]
                tot = na + jnp.dot(qs_scr[d, pl.ds(r0, CHUNK), :], c_st.astype(BF16),
                                   preferred_element_type=F32)
                den = jnp.maximum(jnp.abs(tot[:, dv:dv + 1]), na[:, dv + 1:dv + 2])
                h_scr[d, pl.ds(r0, CHUNK), :] = tot[:, :dv] / den
                nxt.append(ms_scr[d, c, 1:2, :] * c_st + kva_scr[d, pl.ds(r0, CHUNK), :])
            return tuple(nxt)
        return step

    st0 = jnp.zeros((dk, 2 * dv), F32)
    carry = lax.fori_loop(0, ncc, make_step(ncc, 0), (st0, st0), unroll=4)
    lax.fori_loop(0, ncl, make_step(ncl, ncc), carry, unroll=4)

    gn = gn_ref[...]
    for og_ref, h_ref, off, t in ((oc_ref, hc_ref, 0, tc), (ol_ref, hl_ref, tc, tl)):
        for r0 in range(0, t, blk):
            h = h_scr[0, off + r0:off + r0 + blk, :] + h_scr[1, off + r0:off + r0 + blk, :]
            h_ref[r0:r0 + blk, :] = (_rms(h, gn) * _sigmoid(og_ref[r0:r0 + blk, :])).astype(h_ref.dtype)


def _mlstm(p_ctx, p_ml, B, colm, rowsm, scal, gnorm, qk0):
    H = colm.shape[1]
    tc, tl = p_ctx.shape[0] // B, p_ml.shape[0] // B
    tt = tc + tl

    def col(t, c0):
        return pl.BlockSpec((t, HEAD_DIM), lambda b, h: (b, c0 + h))

    def gate(*s):
        return pl.BlockSpec((2, None, None) + s, lambda b, h: (0, h, b) + (0,) * len(s))

    return pl.pallas_call(
        functools.partial(_mlstm_kernel, tc=tc, tl=tl),
        grid=(B, H),
        in_specs=[col(tc, qk0), col(tc, qk0 + H), col(tc, qk0 + 2 * H),
                  col(tl, 0), col(tl, H), col(tl, 2 * H),
                  gate(tt, 2), gate(tt // SUPER, 1, SUPER), gate(tt // CHUNK, 2, 2 * HEAD_DIM),
                  pl.BlockSpec((1, HEAD_DIM), lambda b, h: (0, 0))],
        out_specs=[pl.BlockSpec((tc, HEAD_DIM), lambda b, h: (b, h)),
                   pl.BlockSpec((tl, HEAD_DIM), lambda b, h: (b, h))],
        out_shape=[jax.ShapeDtypeStruct((B * tc, H * HEAD_DIM), BF16),
                   jax.ShapeDtypeStruct((B * tl, H * HEAD_DIM), BF16)],
        scratch_shapes=[pltpu.VMEM((2, tt, 2 * HEAD_DIM), F32)] * 2
                       + [pltpu.VMEM((2, tt, ML_DK), BF16),
                          pltpu.VMEM((2, tt // CHUNK, 3, 2 * HEAD_DIM), F32),
                          pltpu.VMEM((2, tt, HEAD_DIM), F32)],
        compiler_params=_cparams(("parallel", "parallel")),
        name="mlstm",
    )(p_ctx, p_ctx, p_ctx, p_ml, p_ml, p_ml, colm, rowsm, scal, gnorm)


def _in_sizes(D):
    s5 = D // 4
    gw = 3 * D // 8
    gh = gw // HEAD_DIM
    mw = D - s5 - gw
    mh = mw // HEAD_DIM
    return s5, gw, gh, mw, mh


def _permute_w_in(w, D):
    s5, gw, gh, mw, mh = _in_sizes(D)
    o_a = s5 + 4 * gw
    o_mq = o_a + 4 * gh
    o_mk = o_mq + mh * ML_DK
    o_mv = o_mk + mh * ML_DK
    o_mi = o_mv + 2 * mw
    end = o_mi + 4 * mh
    pieces = [w[:, :o_a]]
    for h in range(mh):
        pieces += [w[:, o_mq + h * ML_DK:o_mq + (h + 1) * ML_DK],
                   w[:, o_mk + h * ML_DK:o_mk + (h + 1) * ML_DK]]
    pieces += [w[:, o_mv:o_mi], w[:, o_a:o_mq], w[:, o_mi:end]]
    used = o_a + 2 * mh * ML_DK + 2 * mw + 4 * gh + 4 * mh
    total = -(-used // 512) * 512
    pieces.append(jnp.zeros((w.shape[0], total - used), w.dtype))
    gate0 = o_a + 2 * mh * ML_DK + 2 * mw
    return jnp.concatenate(pieces, axis=1).astype(BF16), gate0


def _token_mixer(p_ctx, p_lat, B, gate0, prm):
    (s5_lam_re, s5_lam_im, s5_log_dt, s5_b_re, s5_b_im, s5_c_re, s5_c_im, s5_d, s5_glu_w,
     s5_glu_b, gdn_conv_w, gdn_a_log, gdn_dt_bias, gdn_norm, ml_i_bias, ml_f_bias, ml_norm) = prm
    tc, tl = p_ctx.shape[0] // B, p_lat.shape[0] // B
    rows = tl // GRID_W
    H = gdn_a_log.shape[-1]
    nch = (tc + tl) // CHUNK

    s5_c, s5_l = _s5_mixer(p_ctx, p_lat, B, s5_lam_re, s5_lam_im, s5_log_dt, s5_b_re, s5_b_im,
                           s5_c_re, s5_c_im, s5_d, s5_glu_w, s5_glu_b)

    def col_major(a):
        return a.reshape(B, rows, GRID_W, -1).swapaxes(1, 2).reshape(B * tl, -1)

    def from_col_major(a):
        return a.reshape(B, GRID_W, rows, -1).swapaxes(1, 2).reshape(B * tl, -1)

    def gate_rows(kind, colmajor):
        c0 = gate0 + kind * 2 * H
        gc = p_ctx[:, c0:c0 + 2 * H]
        gl = p_lat[:, c0:c0 + 2 * H]
        if colmajor:
            gl = col_major(gl)
        g = jnp.concatenate([gc.reshape(B, tc // CHUNK, CHUNK, 2, H),
                             gl.reshape(B, tl // CHUNK, CHUNK, 2, H)], axis=1)
        return g.transpose(3, 4, 0, 1, 2).reshape(2 * H * B * nch, CHUNK)

    def par_rows(p):
        return jnp.broadcast_to(p[:, :, None, None], (2, H, B, nch)).reshape(-1, 1)

    (beta, cum, ecum, edk, egt, bc, r, ew, bt, mc) = _gates(
        gate_rows(0, False), gate_rows(1, False), gate_rows(2, True), gate_rows(3, True),
        par_rows(gdn_a_log), par_rows(gdn_dt_bias), par_rows(ml_i_bias), par_rows(ml_f_bias))

    lead = (2, H, B)
    colg = jnp.stack([beta, cum, ecum, edk], -1).reshape(lead + (nch * CHUNK, 4))
    rowsc = cum.reshape(lead + (nch * CHUNK // SUPER, 1, SUPER))
    egtr = jnp.concatenate([egt, egt], -1).reshape(lead + (nch, 1, HEAD_DIM))
    gd_c, gd_l = _gdn(p_ctx, p_lat, B, gdn_conv_w, colg, rowsc, egtr,
                      gdn_norm.reshape(1, HEAD_DIM))

    colm = jnp.stack([bc, ew], -1).reshape(lead + (nch * CHUNK, 2))
    rowm = r.reshape(lead + (nch * CHUNK // SUPER, 1, SUPER))
    scal = jnp.broadcast_to(jnp.stack([bt[:, :1], mc[:, :1]], 1),
                            (bt.shape[0], 2, 2 * HEAD_DIM)).reshape(lead + (nch, 2, 2 * HEAD_DIM))
    qk0 = gate0 // HEAD_DIM - 3 * H
    p_ml = col_major(p_lat[:, qk0 * HEAD_DIM:gate0])
    ml_c, ml_l = _mlstm(p_ctx, p_ml, B, colm, rowm, scal, ml_norm.reshape(1, HEAD_DIM), qk0)
    ml_l = from_col_major(ml_l)
    return (s5_c, gd_c, ml_c), (s5_l, gd_l, ml_l)


def kernel(x, c, ctx, c_ctx, ada_w, ada_b, norm_mix_pre, norm_mix_post, norm_ffn_pre, norm_ffn_post, w_in, w_out, s5_lam_re, s5_lam_im, s5_log_dt, s5_b_re, s5_b_im, s5_c_re, s5_c_im, s5_d, s5_glu_w, s5_glu_b, gdn_conv_w, gdn_a_log, gdn_dt_bias, gdn_norm, mlstm_i_bias, mlstm_f_bias, mlstm_norm, ffn_w_gate, ffn_w_up, ffn_w_down):
    B, T, D = x.shape
    TC = ctx.shape[1]
    L = ada_w.shape[0]
    assert T % SUPER == 0 and TC % SUPER == 0 and T % GRID_W == 0 and B <= SUBLANES - 1
    s5w, gw, _, _, _ = _in_sizes(D)

    x_lat = x.reshape(B * T, D)
    x_ctx = ctx.reshape(B * TC, D)
    c8 = jnp.concatenate([c, c_ctx[None], jnp.zeros((SUBLANES - B - 1, D), F32)], 0)
    mods = _ada(c8, ada_w, ada_b).reshape(L * SUBLANES, 1, 6 * D)

    tm_lat = 1024
    tm_ctx = min(1024, B * TC)
    tiles_per_batch = T // tm_lat

    for l in range(L):
        lat_row = lambda i, l=l: l * SUBLANES + i // tiles_per_batch
        ctx_row = lambda i, l=l: l * SUBLANES + B
        g_mix_pre = norm_mix_pre[l].reshape(1, D)
        g_mix_post = norm_mix_post[l].reshape(1, D)
        g_ffn_pre = norm_ffn_pre[l].reshape(1, D)
        g_ffn_post = norm_ffn_post[l].reshape(1, D)

        w_in_p, gate0 = _permute_w_in(w_in[l], D)
        p_lat = _inproj(x_lat, mods, lat_row, g_mix_pre, w_in_p, tm_lat, 512)
        p_ctx = _inproj(x_ctx, mods, ctx_row, g_mix_pre, w_in_p, tm_ctx, 512)

        prm = (s5_lam_re[l], s5_lam_im[l], s5_log_dt[l], s5_b_re[l], s5_b_im[l], s5_c_re[l],
               s5_c_im[l], s5_d[l], s5_glu_w[l], s5_glu_b[l], gdn_conv_w[l], gdn_a_log[l],
               gdn_dt_bias[l], gdn_norm[l], mlstm_i_bias[l], mlstm_f_bias[l], mlstm_norm[l])
        mix_ctx, mix_lat = _token_mixer(p_ctx, p_lat, B, gate0, prm)

        wo = w_out[l].astype(BF16)
        wa, wb, wc = wo[:s5w], wo[s5w:s5w + gw], wo[s5w + gw:]
        wg = ffn_w_gate[l].astype(BF16)
        wu = ffn_w_up[l].astype(BF16)
        wd = ffn_w_down[l].astype(BF16)

        lat_row_o = lambda i, l=l: l * SUBLANES + i // (T // 256)
        lat_row_f = lambda i, l=l: l * SUBLANES + i // (T // 512)
        xs = _outproj(x_lat, mix_lat[0], mix_lat[1], mix_lat[2], mods, lat_row_o, g_mix_post,
                      wa, wb, wc, 256)
        x_lat = _ffn(xs, mods, lat_row_f, g_ffn_pre, g_ffn_post, wg, wu, wd, 512, 512)
        if l < L - 1:
            xs = _outproj(x_ctx, mix_ctx[0], mix_ctx[1], mix_ctx[2], mods, ctx_row, g_mix_post,
                          wa, wb, wc, 256)
            x_ctx = _ffn(xs, mods, ctx_row, g_ffn_pre, g_ffn_post, wg, wu, wd, 512, 512)
    return x_lat.reshape(B, T, D)
```

```python
import functools
import math

import jax
import jax.numpy as jnp
from jax import lax
from jax.experimental import pallas as pl
from jax.experimental.pallas import tpu as pltpu

F32 = jnp.float32
BF16 = jnp.bfloat16

EPS = 1e-6
GRID_W = 64
CHUNK = 64
SUPER = 4 * CHUNK
HEAD_DIM = 128
S5_GROUP = 16
S5_STATE = 64
S5_CHUNK = 16
S5_ROW = S5_CHUNK * S5_GROUP
SUBLANES = 8
ML_DK = 64

VMEM_LIMIT = 48 * 1024 * 1024


def _cparams(sem):
    return pltpu.CompilerParams(dimension_semantics=sem, vmem_limit_bytes=VMEM_LIMIT)


def _bdot(a, b, dims=((1,), (0,))):
    return lax.dot_general(a.astype(BF16), b.astype(BF16), (dims, ((), ())),
                           preferred_element_type=F32)


def _split3(a):
    a0 = a.astype(BF16)
    r1 = a - a0.astype(F32)
    a1 = r1.astype(BF16)
    a2 = (r1 - a1.astype(F32)).astype(BF16)
    return a0, a1, a2


def _dot3(a, b):
    a0, a1, _ = _split3(a)
    b0, b1, _ = _split3(b)
    d = functools.partial(jnp.dot, preferred_element_type=F32)
    return d(a0, b0) + (d(a0, b1) + d(a1, b0))


def _dot_exact_rhs(a, b01):
    a0, a1, a2 = _split3(a)
    b = b01.astype(BF16)
    d = functools.partial(jnp.dot, preferred_element_type=F32)
    return d(a0, b) + (d(a1, b) + d(a2, b))


def _sigmoid(x):
    return 1.0 / (1.0 + jnp.exp(-x))


def _silu(x):
    return x * _sigmoid(x)


def _softplus(x):
    return jnp.maximum(x, 0.0) + jnp.log(1.0 + jnp.exp(-jnp.abs(x)))


def _rms(x, g):
    return x * lax.rsqrt(jnp.mean(x * x, axis=-1, keepdims=True) + EPS) * g


def _ada_kernel(c_ref, w_ref, b_ref, o_ref):
    c = c_ref[...]
    o_ref[...] = _bdot(_silu(c), w_ref[...]) + b_ref[...]


def _ada(c8, ada_w, ada_b):
    L, D, N = ada_w.shape
    tn = 1024
    return pl.pallas_call(
        _ada_kernel,
        grid=(L, N // tn),
        in_specs=[pl.BlockSpec((SUBLANES, D), lambda l, j: (0, 0)),
                  pl.BlockSpec((None, D, tn), lambda l, j: (l, 0, j)),
                  pl.BlockSpec((None, 1, tn), lambda l, j: (l, 0, j))],
        out_specs=pl.BlockSpec((None, SUBLANES, tn), lambda l, j: (l, 0, j)),
        out_shape=jax.ShapeDtypeStruct((L, SUBLANES, N), F32),
        compiler_params=_cparams(("parallel", "arbitrary")),
        name="ada",
    )(c8, ada_w, ada_b.reshape(L, 1, N))


def _inproj_kernel(x_ref, sh_ref, sc_ref, g_ref, w_ref, o_ref, h_scr):
    @pl.when(pl.program_id(1) == 0)
    def _():
        h = _rms(x_ref[...], g_ref[...]) * (1.0 + sc_ref[0]) + sh_ref[0]
        h_scr[...] = h.astype(BF16)

    o_ref[...] = jnp.dot(h_scr[...], w_ref[...], preferred_element_type=F32)


def _inproj(x, mods, mod_row, g_pre, w, tm, tn):
    M, D = x.shape
    N = w.shape[1]
    return pl.pallas_call(
        _inproj_kernel,
        grid=(M // tm, N // tn),
        in_specs=[pl.BlockSpec((tm, D), lambda i, j: (i, 0)),
                  pl.BlockSpec((1, 1, D), lambda i, j: (mod_row(i), 0, 0)),
                  pl.BlockSpec((1, 1, D), lambda i, j: (mod_row(i), 0, 1)),
                  pl.BlockSpec((1, D), lambda i, j: (0, 0)),
                  pl.BlockSpec((D, tn), lambda i, j: (0, j))],
        out_specs=pl.BlockSpec((tm, tn), lambda i, j: (i, j)),
        out_shape=jax.ShapeDtypeStruct((M, N), F32),
        scratch_shapes=[pltpu.VMEM((tm, D), BF16)],
        compiler_params=_cparams(("parallel", "arbitrary")),
        name="inproj",
    )(x, mods, mods, g_pre, w)


def _outproj_kernel(x_ref, a_ref, b_ref, c_ref, gt_ref, g_ref, wa_ref, wb_ref, wc_ref, o_ref):
    acc = jnp.dot(a_ref[...], wa_ref[...], preferred_element_type=F32)
    acc += jnp.dot(b_ref[...], wb_ref[...], preferred_element_type=F32)
    acc += jnp.dot(c_ref[...].astype(BF16), wc_ref[...], preferred_element_type=F32)
    o_ref[...] = x_ref[...] + gt_ref[0] * _rms(acc, g_ref[...])


def _outproj(x, ma, mb, mc, mods, mod_row, g_post, wa, wb, wc, tm):
    M, D = x.shape
    row = lambda i: (i, 0)
    full = lambda i: (0, 0)
    return pl.pallas_call(
        _outproj_kernel,
        grid=(M // tm,),
        in_specs=[pl.BlockSpec((tm, D), row),
                  pl.BlockSpec((tm, ma.shape[1]), row),
                  pl.BlockSpec((tm, mb.shape[1]), row),
                  pl.BlockSpec((tm, mc.shape[1]), row),
                  pl.BlockSpec((1, 1, D), lambda i: (mod_row(i), 0, 2)),
                  pl.BlockSpec((1, D), full),
                  pl.BlockSpec(wa.shape, full),
                  pl.BlockSpec(wb.shape, full),
                  pl.BlockSpec(wc.shape, full)],
        out_specs=pl.BlockSpec((tm, D), row),
        out_shape=jax.ShapeDtypeStruct((M, D), F32),
        compiler_params=_cparams(("parallel",)),
        name="outproj",
    )(x, ma, mb, mc, mods, g_post, wa, wb, wc)


def _ffn_kernel(x_ref, sh_ref, sc_ref, gt_ref, gpre_ref, gpost_ref, wg_ref, wu_ref, wd_ref,
                o_ref, h_scr, acc_scr):
    j = pl.program_id(1)

    @pl.when(j == 0)
    def _():
        h = _rms(x_ref[...], gpre_ref[...]) * (1.0 + sc_ref[0]) + sh_ref[0]
        h_scr[...] = h.astype(BF16)
        acc_scr[...] = jnp.zeros_like(acc_scr)

    h = h_scr[...]
    g = jnp.dot(h, wg_ref[...], preferred_element_type=F32)
    u = jnp.dot(h, wu_ref[...], preferred_element_type=F32)
    a = (_silu(g) * u).astype(BF16)
    acc_scr[...] += jnp.dot(a, wd_ref[...], preferred_element_type=F32)

    @pl.when(j == pl.num_programs(1) - 1)
    def _():
        o_ref[...] = x_ref[...] + gt_ref[0] * _rms(acc_scr[...], gpost_ref[...])


def _ffn(x, mods, mod_row, g_pre, g_post, wg, wu, wd, tm, th):
    M, D = x.shape
    H = wg.shape[1]
    row = lambda i, j: (i, 0)
    full = lambda i, j: (0, 0)
    return pl.pallas_call(
        _ffn_kernel,
        grid=(M // tm, H // th),
        in_specs=[pl.BlockSpec((tm, D), row),
                  pl.BlockSpec((1, 1, D), lambda i, j: (mod_row(i), 0, 3)),
                  pl.BlockSpec((1, 1, D), lambda i, j: (mod_row(i), 0, 4)),
                  pl.BlockSpec((1, 1, D), lambda i, j: (mod_row(i), 0, 5)),
                  pl.BlockSpec((1, D), full),
                  pl.BlockSpec((1, D), full),
                  pl.BlockSpec((D, th), lambda i, j: (0, j)),
                  pl.BlockSpec((D, th), lambda i, j: (0, j)),
                  pl.BlockSpec((th, D), lambda i, j: (j, 0))],
        out_specs=pl.BlockSpec((tm, D), row),
        out_shape=jax.ShapeDtypeStruct((M, D), F32),
        scratch_shapes=[pltpu.VMEM((tm, D), BF16), pltpu.VMEM((tm, D), F32)],
        compiler_params=_cparams(("parallel", "arbitrary")),
        name="ffn",
    )(x, mods, mods, mods, g_pre, g_post, wg, wu, wd)


def _s5_param_kernel(lre_ref, lim_ref, ldt_ref, bre_ref, bim_ref, cre_ref, cim_ref,
                     k_ref, care_ref, ncaim_ref, abre_ref, abim_ref, alre_ref, alim_ref):
    dt = jnp.exp(ldt_ref[...])
    lre, lim = lre_ref[...], lim_ref[...]
    mag = jnp.exp(lre * dt)
    ar = mag * jnp.cos(lim * dt)
    ai = mag * jnp.sin(lim * dt)
    den = lre * lre + lim * lim
    f_re = ((ar - 1.0) * lre + ai * lim) / den
    f_im = (ai * lre - (ar - 1.0) * lim) / den
    b_re, b_im = bre_ref[...], bim_ref[...]
    bb_re = f_re * b_re - f_im * b_im
    bb_im = f_re * b_im + f_im * b_re
    c_re, c_im = cre_ref[...], cim_ref[...]

    def lag_kernel(ca, bb):
        return jnp.einsum('ghp,gjp->ghj', ca, bb, precision=lax.Precision.HIGHEST,
                          preferred_element_type=F32)

    pr, pi = jnp.ones_like(ar), jnp.zeros_like(ar)
    for k in range(S5_CHUNK + 1):
        ca_re = c_re * pr - c_im * pi
        ca_im = c_re * pi + c_im * pr
        care_ref[k] = ca_re
        ncaim_ref[k] = -ca_im
        if k < S5_CHUNK:
            k_ref[k] = lag_kernel(ca_re, bb_re) - lag_kernel(ca_im, bb_im)
            abre_ref[k] = pr * bb_re - pi * bb_im
            abim_ref[k] = pr * bb_im + pi * bb_re
        else:
            alre_ref[...] = pr
            alim_ref[...] = pi
        pr, pi = pr * ar - pi * ai, pr * ai + pi * ar


def _s5_params(lam_re, lam_im, log_dt, bt_re, bt_im, c_re, c_im):
    _, G, P = lam_re.shape
    Hg = bt_re.shape[-2]
    Lc = S5_CHUNK
    gb = 8
    par = lambda *s: pl.BlockSpec((None, gb) + s, lambda d, g: (d, g) + (0,) * len(s))
    lag = lambda n, *s: pl.BlockSpec((None, n, gb) + s, lambda d, g: (d, 0, g) + (0,) * len(s))
    return pl.pallas_call(
        _s5_param_kernel,
        grid=(2, G // gb),
        in_specs=[par(1, P), par(1, P), par(1, 1), par(Hg, P), par(Hg, P), par(Hg, P), par(Hg, P)],
        out_specs=[lag(Lc, Hg, Hg), lag(Lc + 1, Hg, P), lag(Lc + 1, Hg, P),
                   lag(Lc, Hg, P), lag(Lc, Hg, P), par(1, P), par(1, P)],
        out_shape=[jax.ShapeDtypeStruct((2, Lc, G, Hg, Hg), F32),
                   jax.ShapeDtypeStruct((2, Lc + 1, G, Hg, P), F32),
                   jax.ShapeDtypeStruct((2, Lc + 1, G, Hg, P), F32),
                   jax.ShapeDtypeStruct((2, Lc, G, Hg, P), F32),
                   jax.ShapeDtypeStruct((2, Lc, G, Hg, P), F32),
                   jax.ShapeDtypeStruct((2, G, 1, P), F32),
                   jax.ShapeDtypeStruct((2, G, 1, P), F32)],
        compiler_params=_cparams(("parallel", "parallel")),
        name="s5_params",
    )(lam_re.reshape(2, G, 1, P), lam_im.reshape(2, G, 1, P), log_dt.reshape(2, G, 1, 1),
      bt_re, bt_im, c_re, c_im)


def _s5_scan_kernel(xc_ref, xl_ref, tt_ref, ore_ref, oim_ref, wre_ref, wim_ref, al_ref,
                    yc_ref, yl_ref, u_scr, y_scr, sin_re, sin_im, sst_re, sst_im,
                    *, nb, nc_ctx, nc_lat):
    rc, rl = nb * nc_ctx, nb * nc_lat
    hg = S5_GROUP
    n_g = xc_ref.shape[-1] // hg
    for gi in range(n_g):
        lo = gi * hg
        u_scr[0:rc, :] = jnp.concatenate([xc_ref[:, t, lo:lo + hg] for t in range(S5_CHUNK)], axis=1)
        u_scr[rc:rc + rl, :] = jnp.concatenate([xl_ref[:, t, lo:lo + hg] for t in range(S5_CHUNK)],
                                               axis=1)
        u = u_scr[...].astype(BF16)
        y = None
        for d in range(2):
            sin_re[...] = jnp.dot(u, wre_ref[gi, d], preferred_element_type=F32)
            sin_im[...] = jnp.dot(u, wim_ref[gi, d], preferred_element_type=F32)
            ar = al_ref[gi, 2 * d:2 * d + 1, :]
            ai = al_ref[gi, 2 * d + 1:2 * d + 2, :]

            def make_body(base, n):
                def body(i, carry):
                    sr, si = carry
                    rows = pl.ds(base + (i if d == 0 else n - 1 - i), nb, stride=n)
                    sst_re[rows, :] = sr
                    sst_im[rows, :] = si
                    return (ar * sr - ai * si + sin_re[rows, :], ar * si + ai * sr + sin_im[rows, :])
                return body

            z = jnp.zeros((nb, S5_STATE), F32)
            carry = lax.fori_loop(0, nc_ctx, make_body(0, nc_ctx), (z, z))
            lax.fori_loop(0, nc_lat, make_body(rc, nc_lat), carry)
            yd = jnp.dot(u, tt_ref[gi, d], preferred_element_type=F32)
            yd += _bdot(sst_re[...], ore_ref[gi, d])
            yd += _bdot(sst_im[...], oim_ref[gi, d])
            y = yd if y is None else y + yd
        y_scr[gi] = y
    for t in range(S5_CHUNK):
        piece = jnp.concatenate([y_scr[gi, :, t * hg:(t + 1) * hg] for gi in range(n_g)], axis=1)
        yc_ref[:, t, :] = piece[:rc]
        yl_ref[:, t, :] = piece[rc:]


def _s5_scan(p_ctx, p_lat, B, tt, ore, oim, wre, wim, al):
    G, P, W, Lc = tt.shape[0], S5_STATE, S5_ROW, S5_CHUNK
    lanes = 128
    gb = lanes // S5_GROUP
    rc, rl = p_ctx.shape[0] // Lc, p_lat.shape[0] // Lc
    xc = p_ctx.reshape(rc, Lc, p_ctx.shape[1])
    xl = p_lat.reshape(rl, Lc, p_lat.shape[1])
    g3 = lambda *s: pl.BlockSpec((gb,) + s, lambda j: (j,) + (0,) * len(s))
    x3 = lambda r: pl.BlockSpec((r, Lc, lanes), lambda j: (0, 0, j))
    R = rc + rl
    yc, yl = pl.pallas_call(
        functools.partial(_s5_scan_kernel, nb=B, nc_ctx=rc // B, nc_lat=rl // B),
        grid=(G // gb,),
        in_specs=[x3(rc), x3(rl), g3(2, W, W), g3(2, P, W), g3(2, P, W), g3(2, W, P), g3(2, W, P),
                  g3(4, P)],
        out_specs=[x3(rc), x3(rl)],
        out_shape=[jax.ShapeDtypeStruct((rc, Lc, G * S5_GROUP), F32),
                   jax.ShapeDtypeStruct((rl, Lc, G * S5_GROUP), F32)],
        scratch_shapes=[pltpu.VMEM((R, W), F32), pltpu.VMEM((gb, R, W), F32)]
                       + [pltpu.VMEM((R, P), F32)] * 4,
        compiler_params=_cparams(("parallel",)),
        name="s5_scan",
    )(xc, xl, tt, ore, oim, wre, wim, al)
    return yc.reshape(rc * Lc, -1), yl.reshape(rl * Lc, -1)


def _s5_out_kernel(y_ref, u_ref, d_ref, w_ref, b_ref, o_ref):
    x = y_ref[...] + d_ref[...] * u_ref[...]
    y = 0.5 * x * (1.0 + jnp.tanh(math.sqrt(2.0 / math.pi) * (x + 0.044715 * (x * x * x))))
    gate = _bdot(y, w_ref[...]) + b_ref[...]
    o_ref[...] = (y * _sigmoid(gate)).astype(o_ref.dtype)


def _s5_out(y, p, d, w, b, tm):
    M, W = y.shape
    row = lambda i: (i, 0)
    full = lambda i: (0, 0)
    return pl.pallas_call(
        _s5_out_kernel,
        grid=(M // tm,),
        in_specs=[pl.BlockSpec((tm, W), row), pl.BlockSpec((tm, W), row),
                  pl.BlockSpec((1, W), full), pl.BlockSpec((W, W), full),
                  pl.BlockSpec((1, W), full)],
        out_specs=pl.BlockSpec((tm, W), row),
        out_shape=jax.ShapeDtypeStruct((M, W), BF16),
        compiler_params=_cparams(("parallel",)),
        name="s5_out",
    )(y, p, d, w, b)


def _s5_mixer(p_ctx, p_lat, B, lam_re, lam_im, log_dt, b_re, b_im, c_re, c_im, d, glu_w, glu_b):
    G, Hg, P, Lc = lam_re.shape[1], S5_GROUP, S5_STATE, S5_CHUNK
    W = G * Hg
    kk, care, ncaim, abre, abim, alre, alim = _s5_params(
        lam_re, lam_im, log_dt, b_re.swapaxes(-1, -2), b_im.swapaxes(-1, -2), c_re, c_im)
    s_i = jnp.arange(Lc)[:, None]
    t_i = jnp.arange(Lc)[None, :]

    def toeplitz(k, lag):
        m = jnp.where((lag >= 0)[:, :, None, None, None], k[jnp.clip(lag, 0, Lc - 1)], 0.0)
        return m.transpose(2, 0, 4, 1, 3).reshape(G, S5_ROW, S5_ROW)

    tt = jnp.stack([toeplitz(kk[0], t_i - s_i), toeplitz(kk[1], s_i - t_i)], 1).astype(BF16)

    def readout(ca):
        f = ca[0, 1:]
        b = ca[1, :0:-1]
        o = jnp.stack([f, b], 0)
        return o.transpose(2, 0, 4, 1, 3).reshape(G, 2, P, S5_ROW).astype(BF16)

    def writein(ab):
        w = jnp.stack([ab[0, ::-1], ab[1]], 0)
        return w.transpose(2, 0, 1, 3, 4).reshape(G, 2, S5_ROW, P).astype(BF16)

    ore, oim = readout(care), readout(ncaim)
    wre, wim = writein(abre), writein(abim)
    al = jnp.stack([alre[0], alim[0], alre[1], alim[1]], 1).reshape(G, 4, P)

    y_c, y_l = _s5_scan(p_ctx, p_lat, B, tt, ore, oim, wre, wim, al)
    dd = d.reshape(1, W)
    gw = glu_w.astype(BF16)
    gb = glu_b.reshape(1, W)
    return (_s5_out(y_c, p_ctx, dd, gw, gb, min(1024, y_c.shape[0])),
            _s5_out(y_l, p_lat, dd, gw, gb, min(1024, y_l.shape[0])))


LANES = 128
GATE_SLOT = 8


def _dot_exact_lhs(b01, a):
    a0, a1, a2 = _split3(a)
    b = b01.astype(BF16)
    d = functools.partial(jnp.dot, preferred_element_type=F32)
    return d(b, a0) + (d(b, a1) + d(b, a2))


def _rt_window(heads):
    start = (6 * heads) // SUBLANES * SUBLANES
    assert 8 * heads <= start + 2 * GATE_SLOT
    return start, 6 * heads - start


def _gates_kernel(xc_ref, xl_ref, pa_ref, pb_ref, gg_ref, gm_ref, cumt_ref, rt_ref, btm_ref,
                  *, tc, tl, heads):
    blk, H = SUPER, heads
    n2 = 2 * H
    per = blk // CHUNK
    cols_per_blk = blk // (tl // GRID_W)
    ri = lax.broadcasted_iota(jnp.int32, (blk, blk), 0)
    ci = lax.broadcasted_iota(jnp.int32, (blk, blk), 1)
    same = (ri // CHUNK) == (ci // CHUNK)
    tri_f = (same & (ri >= ci)).astype(BF16)
    tri_b = (same & (ri <= ci)).astype(BF16)
    fwd_lane = (lax.broadcasted_iota(jnp.int32, (blk, LANES), 1) % n2) < H
    src = lax.broadcasted_iota(jnp.int32, (LANES, LANES), 0)
    dst = lax.broadcasted_iota(jnp.int32, (LANES, LANES), 1)

    def perm(base, q):
        rel = src - base
        tgt = (rel % H) * (2 * GATE_SLOT) + (rel // H) * GATE_SLOT + q
        return ((rel >= 0) & (rel < n2) & (dst == tgt)).astype(BF16)

    i_to_f = ((src >= 2 * n2) & (src < 3 * n2) & (dst == src + n2)).astype(BF16)
    perms_g = [perm(n2, 0)] + [perm(0, q) for q in (1, 2, 3, 4)]
    perms_m = [perm(3 * n2, q) for q in (0, 1, 2)]
    rt0, _ = _rt_window(H)
    pa, pb = pa_ref[...], pb_ref[...]
    zpad = jnp.zeros((blk, LANES - 2 * GATE_SLOT), F32)

    def cumsums(x):
        f = _dot_exact_lhs(tri_f, x)
        b = _dot_exact_lhs(tri_b, x)
        return jnp.where(fwd_lane, f, b), f + b - x

    def place(xs, perms, o_ref):
        out = None
        for x, p in zip(xs, perms):
            t = _dot_exact_rhs(x, p)
            out = t if out is None else out + t
        for h in range(H):
            piece = out[:, h * 2 * GATE_SLOT:(h + 1) * 2 * GATE_SLOT]
            o_ref[:, h * LANES:(h + 1) * LANES] = jnp.concatenate([piece, zpad], axis=1)

    def emit(x_seq, x_scan):
        g = -jnp.exp(pa) * _softplus(x_seq + pb)
        cum, tot = cumsums(g)
        place([_sigmoid(x_seq), cum, jnp.exp(cum), jnp.exp(tot - cum), jnp.exp(tot)], perms_g, gg_ref)
        cumt_ref[...] = cum.T[0:2 * GATE_SLOT, :]
        xb = x_scan + pb
        ig = _dot_exact_rhs(xb, i_to_f)
        b, bt = cumsums(-_softplus(-xb))
        log_w = bt - b + ig
        mch = jnp.max(log_w.reshape(per, CHUNK, LANES), axis=1, keepdims=True)
        ew = jnp.exp(log_w.reshape(per, CHUNK, LANES) - mch).reshape(blk, LANES)
        r = ig - b
        place([b, ew, r], perms_m, gm_ref)
        rt_ref[...] = r.T[rt0:rt0 + 2 * GATE_SLOT, :]
        for j in range(per):
            btm_ref[j, 0:1, :] = bt[j * CHUNK:j * CHUNK + 1, :]
            btm_ref[j, 1:2, :] = mch[j]

    sc = pl.program_id(1)
    n_ctx = tc // blk

    @pl.when(sc < n_ctx)
    def _():
        x = xc_ref[pl.ds(pl.multiple_of(sc * blk, blk), blk), :]
        emit(x, x)

    @pl.when(sc >= n_ctx)
    def _():
        s = sc - n_ctx
        x_seq = xl_ref[pl.ds(pl.multiple_of(s * blk, blk), blk), :]
        x_scan = jnp.concatenate(
            [xl_ref[pl.ds(s * cols_per_blk + j, tl // GRID_W, stride=GRID_W), :]
             for j in range(cols_per_blk)], axis=0)
        emit(x_seq, x_scan)


def _gates(p_ctx, p_lat, B, gate0, pa, pb, heads):
    tc, tl = p_ctx.shape[0] // B, p_lat.shape[0] // B
    tt = tc + tl
    nsc = tt // SUPER
    per = SUPER // CHUNK
    gblk = gate0 // LANES
    tok = pl.BlockSpec((None, SUPER, heads * LANES), lambda b, s: (b, s, 0))
    rowf = pl.BlockSpec((None, None, 2 * GATE_SLOT, SUPER), lambda b, s: (b, s, 0, 0))
    return pl.pallas_call(
        functools.partial(_gates_kernel, tc=tc, tl=tl, heads=heads),
        grid=(B, nsc),
        in_specs=[pl.BlockSpec((tc, LANES), lambda b, s: (b, gblk)),
                  pl.BlockSpec((tl, LANES), lambda b, s: (b, gblk)),
                  pl.BlockSpec((1, LANES), lambda b, s: (0, 0)),
                  pl.BlockSpec((1, LANES), lambda b, s: (0, 0))],
        out_specs=[tok, tok, rowf, rowf,
                   pl.BlockSpec((None, per, 2, LANES), lambda b, s: (b, s, 0, 0))],
        out_shape=[jax.ShapeDtypeStruct((B, tt, heads * LANES), F32),
                   jax.ShapeDtypeStruct((B, tt, heads * LANES), F32),
                   jax.ShapeDtypeStruct((B, nsc, 2 * GATE_SLOT, SUPER), F32),
                   jax.ShapeDtypeStruct((B, nsc, 2 * GATE_SLOT, SUPER), F32),
                   jax.ShapeDtypeStruct((B, tt // CHUNK, 2, LANES), F32)],
        compiler_params=_cparams(("parallel", "parallel")),
        name="gates",
    )(p_ctx, p_lat, pa, pb)


def _gdn_kernel(qc_ref, kc_ref, vc_ref, zc_ref, ql_ref, kl_ref, vl_ref, zl_ref,
                cwq_ref, cwk_ref, cwv_ref, gg_ref, cumt_ref, gn_ref,
                oc_ref, ol_ref, pad_scr, qn, kn, vn, a_scr, b_scr, qp_scr, op_scr, o_scr,
                *, tc, tl, heads):
    dk = HEAD_DIM
    blk = SUPER
    head = pl.program_id(1)

    def conv(src_ref, w_ref, dst, off, t, norm):
        pad_scr[0:8, :] = jnp.zeros((8, dk), F32)
        pad_scr[8:8 + t, :] = src_ref[...]
        pad_scr[8 + t:16 + t, :] = jnp.zeros((8, dk), F32)
        w0, w1, w2 = w_ref[0:1, :], w_ref[1:2, :], w_ref[2:3, :]
        for r0 in range(0, t, blk):
            y = (w0 * pad_scr[r0 + 7:r0 + 7 + blk, :] + w1 * pad_scr[r0 + 8:r0 + 8 + blk, :]
                 + w2 * pad_scr[r0 + 9:r0 + 9 + blk, :])
            y = _silu(y)
            if norm is not None:
                y = y * lax.rsqrt(jnp.sum(y * y, axis=-1, keepdims=True) + EPS) * norm
            dst[off + r0:off + r0 + blk, :] = y

    for src_c, src_l, w_ref, dst, norm in ((qc_ref, ql_ref, cwq_ref, qn, dk ** -0.5),
                                           (kc_ref, kl_ref, cwk_ref, kn, 1.0),
                                           (vc_ref, vl_ref, cwv_ref, vn, None)):
        conv(src_c, w_ref, dst, 0, tc, norm)
        conv(src_l, w_ref, dst, tc, tl, norm)

    n_sc = (tc + tl) // blk
    ncc, ncl = tc // CHUNK, tl // CHUNK
    per = blk // CHUNK
    nt = ((1,), (1,))

    ri = lax.broadcasted_iota(jnp.int32, (blk, blk), 0)
    ci = lax.broadcasted_iota(jnp.int32, (blk, blk), 1)
    same = (ri // CHUNK) == (ci // CHUNK)
    eye_b = (ri == ci).astype(F32)
    tri_m = (same & (ri >= ci), same & (ri <= ci))
    strict_m = (same & (ri > ci), same & (ri < ci))
    lr = lax.broadcasted_iota(jnp.int32, (per * dk, blk), 0)
    lc = lax.broadcasted_iota(jnp.int32, (per * dk, blk), 1)
    chunk_sel = (lr // dk) == (lc // CHUNK)

    def prepare(sc, _):
        r0 = pl.multiple_of(sc * blk, blk)
        q = qn[pl.ds(r0, blk), :]
        k = kn[pl.ds(r0, blk), :]
        v = vn[pl.ds(r0, blk), :]
        qkt = _bdot(q, k, nt)
        dirs = (0, 1)
        cg = gg_ref[pl.ds(r0, blk), :]
        cgs = [cg[:, d * GATE_SLOT:(d + 1) * GATE_SLOT] for d in dirs]
        crows = [cumt_ref[sc, pl.ds(d * heads + head, 1), :] for d in dirs]
        decs = [jnp.where(tri_m[d], jnp.exp(jnp.where(tri_m[d], cgs[d][:, 1:2] - crows[d], 0.0)),
                          0.0) for d in dirs]
        kbs = [k * cgs[d][:, 0:1] for d in dirs]
        mps = [jnp.where(strict_m[d], _bdot(kbs[d], k, nt) * decs[d], 0.0) for d in dirs]
        ps = [eye_b - mp for mp in mps]
        for _ in range(5):
            mps = [_bdot(mp, mp) for mp in mps]
            ps = [p + _bdot(p, mp) for p, mp in zip(ps, mps)]
        uws = [_bdot(ps[d], jnp.concatenate([v * cgs[d][:, 0:1], kbs[d] * cgs[d][:, 2:3]], axis=1))
               for d in dirs]
        ows = [_bdot(jnp.where(tri_m[d], qkt * decs[d], 0.0), uws[d]) for d in dirs]
        lhss = [jnp.where(chunk_sel, jnp.concatenate([(k * cgs[d][:, 3:4]).T] * per, axis=0), 0.0)
                for d in dirs]
        abs_ = [_bdot(lhss[d], uws[d]) for d in dirs]
        ra = pl.multiple_of(sc * (per * dk), per * dk)
        for d in dirs:
            qp_scr[d, pl.ds(r0, blk), :] = (q * cgs[d][:, 2:3] - ows[d][:, dk:]).astype(BF16)
            op_scr[d, pl.ds(r0, blk), :] = ows[d][:, :dk]
            a_scr[d, pl.ds(ra, per * dk), :] = abs_[d][:, dk:].astype(BF16)
            b_scr[d, pl.ds(ra, per * dk), :] = abs_[d][:, :dk]
        return 0

    lax.fori_loop(0, n_sc, prepare, 0)

    def make_step(n, off_c):
        def step(i, carry):
            nxt = []
            for d in range(2):
                s = carry[d]
                c = off_c + (i if d == 0 else n - 1 - i)
                r0 = pl.multiple_of(c * CHUNK, CHUNK)
                ra = pl.multiple_of(c * dk, dk)
                sb = s.astype(BF16)
                o = jnp.dot(qp_scr[d, pl.ds(r0, CHUNK), :], sb, preferred_element_type=F32)
                o_scr[d, pl.ds(r0, CHUNK), :] = o + op_scr[d, pl.ds(r0, CHUNK), :]
                egt = gg_ref[pl.ds(r0, 1), :][:, d * GATE_SLOT + 4:d * GATE_SLOT + 5]
                nxt.append(s * egt + b_scr[d, pl.ds(ra, dk), :]
                           - jnp.dot(a_scr[d, pl.ds(ra, dk), :], sb, preferred_element_type=F32))
            return tuple(nxt)
        return step

    s0 = jnp.zeros((dk, dk), F32)
    carry = lax.fori_loop(0, ncc, make_step(ncc, 0), (s0, s0), unroll=2)
    lax.fori_loop(0, ncl, make_step(ncl, ncc), carry, unroll=2)

    gn = gn_ref[...]
    for z_ref, o_ref, off, t in ((zc_ref, oc_ref, 0, tc), (zl_ref, ol_ref, tc, tl)):
        for r0 in range(0, t, blk):
            o = o_scr[0, off + r0:off + r0 + blk, :] + o_scr[1, off + r0:off + r0 + blk, :]
            o_ref[r0:r0 + blk, :] = (_rms(o, gn) * _silu(z_ref[r0:r0 + blk, :])).astype(o_ref.dtype)


def _gdn(p_ctx, p_lat, B, conv_w, gg, cumt, gnorm, H):
    tc, tl = p_ctx.shape[0] // B, p_lat.shape[0] // B
    tt = tc + tl
    q0, k0, v0, z0 = 4, 4 + H, 4 + 2 * H, 4 + 3 * H

    def col(t, c0):
        return pl.BlockSpec((t, HEAD_DIM), lambda b, h: (b, c0 + h))

    return pl.pallas_call(
        functools.partial(_gdn_kernel, tc=tc, tl=tl, heads=H),
        grid=(B, H),
        in_specs=[col(tc, q0), col(tc, k0), col(tc, v0), col(tc, z0),
                  col(tl, q0), col(tl, k0), col(tl, v0), col(tl, z0),
                  pl.BlockSpec((3, HEAD_DIM), lambda b, h: (0, h)),
                  pl.BlockSpec((3, HEAD_DIM), lambda b, h: (0, H + h)),
                  pl.BlockSpec((3, HEAD_DIM), lambda b, h: (0, 2 * H + h)),
                  pl.BlockSpec((None, tt, LANES), lambda b, h: (b, 0, h)),
                  pl.BlockSpec((None,) + cumt.shape[1:], lambda b, h: (b, 0, 0, 0)),
                  pl.BlockSpec((1, HEAD_DIM), lambda b, h: (0, 0))],
        out_specs=[pl.BlockSpec((tc, HEAD_DIM), lambda b, h: (b, h)),
                   pl.BlockSpec((tl, HEAD_DIM), lambda b, h: (b, h))],
        out_shape=[jax.ShapeDtypeStruct((B * tc, H * HEAD_DIM), BF16),
                   jax.ShapeDtypeStruct((B * tl, H * HEAD_DIM), BF16)],
        scratch_shapes=[pltpu.VMEM((max(tc, tl) + 16, HEAD_DIM), F32)]
                       + [pltpu.VMEM((tt, HEAD_DIM), F32)] * 3
                       + [pltpu.VMEM((2, tt // CHUNK * HEAD_DIM, HEAD_DIM), BF16),
                          pltpu.VMEM((2, tt // CHUNK * HEAD_DIM, HEAD_DIM), F32),
                          pltpu.VMEM((2, tt, HEAD_DIM), BF16),
                          pltpu.VMEM((2, tt, HEAD_DIM), F32),
                          pltpu.VMEM((2, tt, HEAD_DIM), F32)],
        compiler_params=_cparams(("parallel", "parallel")),
        name="gdn",
    )(p_ctx, p_ctx, p_ctx, p_ctx, p_lat, p_lat, p_lat, p_lat, conv_w, conv_w, conv_w,
      gg, cumt, gnorm)


def _mlstm_kernel(qkc_ref, vc_ref, oc_ref, qkl_ref, vl_ref, ol_ref, gm_ref, rt_ref,
                  btm_ref, gn_ref, hc_ref, hl_ref, na_scr, kva_scr, qs_scr, ms_scr, h_scr, sc_scr,
                  *, tc, tl, heads):
    blk, dk, dv = SUPER, ML_DK, HEAD_DIM
    per = blk // CHUNK
    ncc, ncl = tc // CHUNK, tl // CHUNK
    qscale = dk ** -0.5
    head = pl.program_id(1)
    rt_off = _rt_window(heads)[1]
    grid_rows = tl // GRID_W
    cols_per_blk = blk // grid_rows

    def seq_rows(ref, s):
        return ref[pl.ds(pl.multiple_of(s * blk, blk), blk), :]

    def grid_col_rows(ref, s):
        return jnp.concatenate([ref[pl.ds(s * cols_per_blk + j, grid_rows, stride=GRID_W), :]
                                for j in range(cols_per_blk)], axis=0)

    lane_n = lax.broadcasted_iota(jnp.int32, (ncc + ncl, LANES), 1)
    for d in range(2):
        pick = lane_n == 6 * heads + d * heads + head
        for j in range(2):
            col = jnp.sum(jnp.where(pick, btm_ref[:, j, :], 0.0), axis=1, keepdims=True)
            sc_scr[d, j] = jnp.broadcast_to(col, (ncc + ncl, 2 * dv))

    ri = lax.broadcasted_iota(jnp.int32, (blk, blk), 0)
    ci = lax.broadcasted_iota(jnp.int32, (blk, blk), 1)
    same = (ri // CHUNK) == (ci // CHUNK)
    tri_m = (same & (ri >= ci), same & (ri <= ci))
    lane2 = lax.broadcasted_iota(jnp.int32, (blk, 2 * dv), 1)
    one_col = (lax.broadcasted_iota(jnp.int32, (blk, dv), 1) == 0).astype(F32)

    for d in range(2):
        def make_scalar_step(n, off_c):
            def step(i, m_st):
                c = off_c + (i if d == 0 else n - 1 - i)
                bt = sc_scr[d, 0, pl.ds(c, 1), :]
                mch = sc_scr[d, 1, pl.ds(c, 1), :]
                m_new = jnp.maximum(bt + m_st, mch)
                ms_scr[d, c, 0:1, :] = m_st
                ms_scr[d, c, 1:2, :] = jnp.exp(bt + m_st - m_new)
                ms_scr[d, c, 2:3, :] = jnp.exp(mch - m_new)
                return m_new
            return step
        m_c = lax.fori_loop(0, ncc, make_scalar_step(ncc, 0), jnp.zeros((1, 2 * dv), F32))
        lax.fori_loop(0, ncl, make_scalar_step(ncl, ncc), m_c)

    rblk = lax.broadcasted_iota(jnp.int32, (blk, 1), 0) // CHUNK

    def per_chunk_col(vals):
        col = vals[per - 1]
        for j in range(per - 2, -1, -1):
            col = jnp.where(rblk == j, vals[j], col)
        return col

    def make_prepare(qk_ref, v_ref, off_sc, rows):
        def prepare(s, _):
            sc = off_sc + s
            r0 = pl.multiple_of(sc * blk, blk)
            qk = rows(qk_ref, s)
            q = qk[:, :dk] * qscale
            k = qk[:, dk:]
            va = jnp.concatenate([rows(v_ref, s), one_col], axis=1)
            qkt = _bdot(q, k, ((1,), (1,)))
            cm = gm_ref[pl.ds(r0, blk), :]
            for d in range(2):
                b_col = cm[:, d * GATE_SLOT:d * GATE_SLOT + 1]
                ew = cm[:, d * GATE_SLOT + 1:d * GATE_SLOT + 2]
                r_row = rt_ref[sc, pl.ds(rt_off + d * heads + head, 1), :]
                ms = [ms_scr[d, sc * per + j] for j in range(per)]
                m_st = per_chunk_col([m[0:1, 0:1] for m in ms])
                a_new = per_chunk_col([m[2:3, 0:1] for m in ms])
                log_d = jnp.where(tri_m[d], b_col + r_row, -jnp.inf)
                m_t = jnp.maximum(b_col + m_st, jnp.max(log_d, axis=-1, keepdims=True))
                na = _bdot(qkt * jnp.exp(log_d - m_t), va)
                na_scr[d, pl.ds(r0, blk), :] = jnp.where(lane2 == dv + 1, jnp.exp(-m_t), na)
                qs_scr[d, pl.ds(r0, blk), :] = (q * jnp.exp(b_col + m_st - m_t)).astype(BF16)
                wkt = (k * (ew * a_new)).T
                lhs = jnp.where(same, jnp.concatenate([wkt] * per, axis=0), 0.0)
                kva_scr[d, pl.ds(r0, blk), :] = _bdot(lhs, va)
            return 0
        return prepare

    lax.fori_loop(0, tc // blk, make_prepare(qkc_ref, vc_ref, 0, seq_rows), 0)
    lax.fori_loop(0, tl // blk, make_prepare(qkl_ref, vl_ref, tc // blk, grid_col_rows), 0)

    def make_step(n, off_c):
        def step(i, carry):
            nxt = []
            for d in range(2):
                c_st = carry[d]
                c = off_c + (i if d == 0 else n - 1 - i)
                r0 = pl.multiple_of(c * CHUNK, CHUNK)
                na = na_scr[d, pl.ds(r0, CHUNK), :]
                tot = na + jnp.dot(qs_scr[d, pl.ds(r0, CHUNK), :], c_st.astype(BF16),
                                   preferred_element_type=F32)
                den = jnp.maximum(jnp.abs(tot[:, dv:dv + 1]), na[:, dv + 1:dv + 2])
                h_scr[d, pl.ds(r0, CHUNK), :] = tot[:, :dv] / den
                nxt.append(ms_scr[d, c, 1:2, :] * c_st + kva_scr[d, pl.ds(r0, CHUNK), :])
            return tuple(nxt)
        return step

    st0 = jnp.zeros((dk, 2 * dv), F32)
    carry = lax.fori_loop(0, ncc, make_step(ncc, 0), (st0, st0), unroll=4)
    lax.fori_loop(0, ncl, make_step(ncl, ncc), carry, unroll=4)

    gn = gn_ref[...]

    def gated(r0, og):
        h = h_scr[0, r0:r0 + blk, :] + h_scr[1, r0:r0 + blk, :]
        return (_rms(h, gn) * _sigmoid(og)).astype(hc_ref.dtype)

    for s in range(tc // blk):
        hc_ref[s * blk:(s + 1) * blk, :] = gated(s * blk, oc_ref[s * blk:(s + 1) * blk, :])
    for s in range(tl // blk):
        res = gated(tc + s * blk, grid_col_rows(ol_ref, s))
        for j in range(cols_per_blk):
            hl_ref[pl.ds(s * cols_per_blk + j, grid_rows, stride=GRID_W), :] = (
                res[j * grid_rows:(j + 1) * grid_rows])


def _mlstm(p_ctx, p_lat, B, gm, rt, btm, gnorm, qk0, H):
    tc, tl = p_ctx.shape[0] // B, p_lat.shape[0] // B
    tt = tc + tl

    def col(t, c0):
        return pl.BlockSpec((t, HEAD_DIM), lambda b, h: (b, c0 + h))

    return pl.pallas_call(
        functools.partial(_mlstm_kernel, tc=tc, tl=tl, heads=H),
        grid=(B, H),
        in_specs=[col(tc, qk0), col(tc, qk0 + H), col(tc, qk0 + 2 * H),
                  col(tl, qk0), col(tl, qk0 + H), col(tl, qk0 + 2 * H),
                  pl.BlockSpec((None, tt, LANES), lambda b, h: (b, 0, h)),
                  pl.BlockSpec((None,) + rt.shape[1:], lambda b, h: (b, 0, 0, 0)),
                  pl.BlockSpec((None,) + btm.shape[1:], lambda b, h: (b, 0, 0, 0)),
                  pl.BlockSpec((1, HEAD_DIM), lambda b, h: (0, 0))],
        out_specs=[pl.BlockSpec((tc, HEAD_DIM), lambda b, h: (b, h)),
                   pl.BlockSpec((tl, HEAD_DIM), lambda b, h: (b, h))],
        out_shape=[jax.ShapeDtypeStruct((B * tc, H * HEAD_DIM), F32),
                   jax.ShapeDtypeStruct((B * tl, H * HEAD_DIM), F32)],
        scratch_shapes=[pltpu.VMEM((2, tt, 2 * HEAD_DIM), F32)] * 2
                       + [pltpu.VMEM((2, tt, ML_DK), BF16),
                          pltpu.VMEM((2, tt // CHUNK, 3, 2 * HEAD_DIM), F32),
                          pltpu.VMEM((2, tt, HEAD_DIM), F32),
                          pltpu.VMEM((2, 2, tt // CHUNK, 2 * HEAD_DIM), F32)],
        compiler_params=_cparams(("parallel", "parallel")),
        name="mlstm",
    )(p_ctx, p_ctx, p_ctx, p_lat, p_lat, p_lat, gm, rt, btm, gnorm)


def _in_sizes(D):
    s5 = D // 4
    gw = 3 * D // 8
    gh = gw // HEAD_DIM
    mw = D - s5 - gw
    mh = mw // HEAD_DIM
    return s5, gw, gh, mw, mh


def _permute_w_in(w, D):
    s5, gw, gh, mw, mh = _in_sizes(D)
    o_a = s5 + 4 * gw
    o_mq = o_a + 4 * gh
    o_mk = o_mq + mh * ML_DK
    o_mv = o_mk + mh * ML_DK
    o_mi = o_mv + 2 * mw
    end = o_mi + 4 * mh
    pieces = [w[:, :o_a]]
    for h in range(mh):
        pieces += [w[:, o_mq + h * ML_DK:o_mq + (h + 1) * ML_DK],
                   w[:, o_mk + h * ML_DK:o_mk + (h + 1) * ML_DK]]
    pieces += [w[:, o_mv:o_mi], w[:, o_a:o_mq], w[:, o_mi:end]]
    used = o_a + 2 * mh * ML_DK + 2 * mw + 4 * gh + 4 * mh
    total = -(-used // 512) * 512
    pieces.append(jnp.zeros((w.shape[0], total - used), w.dtype))
    gate0 = o_a + 2 * mh * ML_DK + 2 * mw
    return jnp.concatenate(pieces, axis=1).astype(BF16), gate0


def _token_mixer(p_ctx, p_lat, B, gate0, prm):
    (s5_lam_re, s5_lam_im, s5_log_dt, s5_b_re, s5_b_im, s5_c_re, s5_c_im, s5_d, s5_glu_w,
     s5_glu_b, gdn_conv_w, gdn_a_log, gdn_dt_bias, gdn_norm, ml_i_bias, ml_f_bias, ml_norm) = prm
    tc, tl = p_ctx.shape[0] // B, p_lat.shape[0] // B
    rows = tl // GRID_W
    H = gdn_a_log.shape[-1]
    nch = (tc + tl) // CHUNK

    s5_c, s5_l = _s5_mixer(p_ctx, p_lat, B, s5_lam_re, s5_lam_im, s5_log_dt, s5_b_re, s5_b_im,
                           s5_c_re, s5_c_im, s5_d, s5_glu_w, s5_glu_b)

    n2 = 2 * H
    zero = jnp.zeros((n2,), F32)
    tail = jnp.zeros((LANES - 4 * n2,), F32)
    pa = jnp.concatenate([gdn_a_log.reshape(-1), zero, zero, zero, tail]).reshape(1, LANES)
    pb = jnp.concatenate([gdn_dt_bias.reshape(-1), zero, ml_i_bias.reshape(-1),
                          ml_f_bias.reshape(-1), tail]).reshape(1, LANES)
    gg, gm, cumt, rt, btm = _gates(p_ctx, p_lat, B, gate0, pa, pb, H)

    gd_c, gd_l = _gdn(p_ctx, p_lat, B, gdn_conv_w, gg, cumt, gdn_norm.reshape(1, HEAD_DIM), H)
    qk0 = gate0 // HEAD_DIM - 3 * H
    ml_c, ml_l = _mlstm(p_ctx, p_lat, B, gm, rt, btm, ml_norm.reshape(1, HEAD_DIM), qk0, H)
    return (s5_c, gd_c, ml_c), (s5_l, gd_l, ml_l)


def kernel(x, c, ctx, c_ctx, ada_w, ada_b, norm_mix_pre, norm_mix_post, norm_ffn_pre, norm_ffn_post, w_in, w_out, s5_lam_re, s5_lam_im, s5_log_dt, s5_b_re, s5_b_im, s5_c_re, s5_c_im, s5_d, s5_glu_w, s5_glu_b, gdn_conv_w, gdn_a_log, gdn_dt_bias, gdn_norm, mlstm_i_bias, mlstm_f_bias, mlstm_norm, ffn_w_gate, ffn_w_up, ffn_w_down):
    B, T, D = x.shape
    TC = ctx.shape[1]
    L = ada_w.shape[0]
    assert T % SUPER == 0 and TC % SUPER == 0 and T % GRID_W == 0 and B <= SUBLANES - 1
    s5w, gw, _, _, _ = _in_sizes(D)

    x_lat = x.reshape(B * T, D)
    x_ctx = ctx.reshape(B * TC, D)
    c8 = jnp.concatenate([c, c_ctx[None], jnp.zeros((SUBLANES - B - 1, D), F32)], 0)
    mods = _ada(c8, ada_w, ada_b).reshape(L * SUBLANES, 1, 6 * D)

    tm_lat = 1024
    tm_ctx = min(1024, B * TC)
    tiles_per_batch = T // tm_lat

    for l in range(L):
        lat_row = lambda i, l=l: l * SUBLANES + i // tiles_per_batch
        ctx_row = lambda i, l=l: l * SUBLANES + B
        g_mix_pre = norm_mix_pre[l].reshape(1, D)
        g_mix_post = norm_mix_post[l].reshape(1, D)
        g_ffn_pre = norm_ffn_pre[l].reshape(1, D)
        g_ffn_post = norm_ffn_post[l].reshape(1, D)

        w_in_p, gate0 = _permute_w_in(w_in[l], D)
        p_lat = _inproj(x_lat, mods, lat_row, g_mix_pre, w_in_p, tm_lat, 512)
        p_ctx = _inproj(x_ctx, mods, ctx_row, g_mix_pre, w_in_p, tm_ctx, 512)

        prm = (s5_lam_re[l], s5_lam_im[l], s5_log_dt[l], s5_b_re[l], s5_b_im[l], s5_c_re[l],
               s5_c_im[l], s5_d[l], s5_glu_w[l], s5_glu_b[l], gdn_conv_w[l], gdn_a_log[l],
               gdn_dt_bias[l], gdn_norm[l], mlstm_i_bias[l], mlstm_f_bias[l], mlstm_norm[l])
        mix_ctx, mix_lat = _token_mixer(p_ctx, p_lat, B, gate0, prm)

        wo = w_out[l].astype(BF16)
        wa, wb, wc = wo[:s5w], wo[s5w:s5w + gw], wo[s5w + gw:]
        wg = ffn_w_gate[l].astype(BF16)
        wu = ffn_w_up[l].astype(BF16)
        wd = ffn_w_down[l].astype(BF16)

        lat_row_o = lambda i, l=l: l * SUBLANES + i // (T // 256)
        lat_row_f = lambda i, l=l: l * SUBLANES + i // (T // 512)
        xs = _outproj(x_lat, mix_lat[0], mix_lat[1], mix_lat[2], mods, lat_row_o, g_mix_post,
                      wa, wb, wc, 256)
        x_lat = _ffn(xs, mods, lat_row_f, g_ffn_pre, g_ffn_post, wg, wu, wd, 512, 512)
        if l < L - 1:
            xs = _outproj(x_ctx, mix_ctx[0], mix_ctx[1], mix_ctx[2], mods, ctx_row, g_mix_post,
                          wa, wb, wc, 256)
            x_ctx = _ffn(xs, mods, ctx_row, g_ffn_pre, g_ffn_post, wg, wu, wd, 512, 512)
    return x_lat.reshape(B, T, D)
```

```python
import functools
import math

import jax
import jax.numpy as jnp
from jax import lax
from jax.experimental import pallas as pl
from jax.experimental.pallas import tpu as pltpu

F32 = jnp.float32
BF16 = jnp.bfloat16

EPS = 1e-6
GRID_W = 64
CHUNK = 64
SUPER = 4 * CHUNK
HEAD_DIM = 128
S5_GROUP = 16
S5_STATE = 64
S5_CHUNK = 16
S5_ROW = S5_CHUNK * S5_GROUP
SUBLANES = 8
ML_DK = 64

VMEM_LIMIT = 48 * 1024 * 1024


def _cparams(sem):
    return pltpu.CompilerParams(dimension_semantics=sem, vmem_limit_bytes=VMEM_LIMIT)


def _bdot(a, b, dims=((1,), (0,))):
    return lax.dot_general(a.astype(BF16), b.astype(BF16), (dims, ((), ())),
                           preferred_element_type=F32)


def _split3(a):
    a0 = a.astype(BF16)
    r1 = a - a0.astype(F32)
    a1 = r1.astype(BF16)
    a2 = (r1 - a1.astype(F32)).astype(BF16)
    return a0, a1, a2


def _dot3(a, b):
    a0, a1, _ = _split3(a)
    b0, b1, _ = _split3(b)
    d = functools.partial(jnp.dot, preferred_element_type=F32)
    return d(a0, b0) + (d(a0, b1) + d(a1, b0))


def _dot_exact_rhs(a, b01):
    a0, a1, a2 = _split3(a)
    b = b01.astype(BF16)
    d = functools.partial(jnp.dot, preferred_element_type=F32)
    return d(a0, b) + (d(a1, b) + d(a2, b))


def _sigmoid(x):
    return 1.0 / (1.0 + jnp.exp(-x))


def _silu(x):
    return x * _sigmoid(x)


def _softplus(x):
    return jnp.maximum(x, 0.0) + jnp.log(1.0 + jnp.exp(-jnp.abs(x)))


def _rms(x, g):
    return x * lax.rsqrt(jnp.mean(x * x, axis=-1, keepdims=True) + EPS) * g


def _ada_kernel(c_ref, w_ref, b_ref, o_ref):
    c = c_ref[...]
    o_ref[...] = _bdot(_silu(c), w_ref[...]) + b_ref[...]


def _ada(c8, ada_w, ada_b):
    L, D, N = ada_w.shape
    tn = 1024
    return pl.pallas_call(
        _ada_kernel,
        grid=(L, N // tn),
        in_specs=[pl.BlockSpec((SUBLANES, D), lambda l, j: (0, 0)),
                  pl.BlockSpec((None, D, tn), lambda l, j: (l, 0, j)),
                  pl.BlockSpec((None, 1, tn), lambda l, j: (l, 0, j))],
        out_specs=pl.BlockSpec((None, SUBLANES, tn), lambda l, j: (l, 0, j)),
        out_shape=jax.ShapeDtypeStruct((L, SUBLANES, N), F32),
        compiler_params=_cparams(("parallel", "arbitrary")),
        name="ada",
    )(c8, ada_w, ada_b.reshape(L, 1, N))


def _inproj_kernel(x_ref, sh_ref, sc_ref, g_ref, w_ref, o_ref, h_scr):
    @pl.when(pl.program_id(1) == 0)
    def _():
        h = _rms(x_ref[...], g_ref[...]) * (1.0 + sc_ref[0]) + sh_ref[0]
        h_scr[...] = h.astype(BF16)

    o_ref[...] = jnp.dot(h_scr[...], w_ref[...], preferred_element_type=F32)


def _inproj(x, mods, mod_row, g_pre, w, tm, tn):
    M, D = x.shape
    N = w.shape[1]
    return pl.pallas_call(
        _inproj_kernel,
        grid=(M // tm, N // tn),
        in_specs=[pl.BlockSpec((tm, D), lambda i, j: (i, 0)),
                  pl.BlockSpec((1, 1, D), lambda i, j: (mod_row(i), 0, 0)),
                  pl.BlockSpec((1, 1, D), lambda i, j: (mod_row(i), 0, 1)),
                  pl.BlockSpec((1, D), lambda i, j: (0, 0)),
                  pl.BlockSpec((D, tn), lambda i, j: (0, j))],
        out_specs=pl.BlockSpec((tm, tn), lambda i, j: (i, j)),
        out_shape=jax.ShapeDtypeStruct((M, N), F32),
        scratch_shapes=[pltpu.VMEM((tm, D), BF16)],
        compiler_params=_cparams(("parallel", "arbitrary")),
        name="inproj",
    )(x, mods, mods, g_pre, w)


def _outproj_kernel(x_ref, a_ref, b_ref, c_ref, gt_ref, g_ref, wa_ref, wb_ref, wc_ref, o_ref):
    acc = jnp.dot(a_ref[...], wa_ref[...], preferred_element_type=F32)
    acc += jnp.dot(b_ref[...], wb_ref[...], preferred_element_type=F32)
    acc += jnp.dot(c_ref[...].astype(BF16), wc_ref[...], preferred_element_type=F32)
    o_ref[...] = x_ref[...] + gt_ref[0] * _rms(acc, g_ref[...])


def _outproj(x, ma, mb, mc, mods, mod_row, g_post, wa, wb, wc, tm):
    M, D = x.shape
    row = lambda i: (i, 0)
    full = lambda i: (0, 0)
    return pl.pallas_call(
        _outproj_kernel,
        grid=(M // tm,),
        in_specs=[pl.BlockSpec((tm, D), row),
                  pl.BlockSpec((tm, ma.shape[1]), row),
                  pl.BlockSpec((tm, mb.shape[1]), row),
                  pl.BlockSpec((tm, mc.shape[1]), row),
                  pl.BlockSpec((1, 1, D), lambda i: (mod_row(i), 0, 2)),
                  pl.BlockSpec((1, D), full),
                  pl.BlockSpec(wa.shape, full),
                  pl.BlockSpec(wb.shape, full),
                  pl.BlockSpec(wc.shape, full)],
        out_specs=pl.BlockSpec((tm, D), row),
        out_shape=jax.ShapeDtypeStruct((M, D), F32),
        compiler_params=_cparams(("parallel",)),
        name="outproj",
    )(x, ma, mb, mc, mods, g_post, wa, wb, wc)


def _ffn_kernel(x_ref, sh_ref, sc_ref, gt_ref, gpre_ref, gpost_ref, wg_ref, wu_ref, wd_ref,
                o_ref, h_scr, acc_scr):
    j = pl.program_id(1)

    @pl.when(j == 0)
    def _():
        h = _rms(x_ref[...], gpre_ref[...]) * (1.0 + sc_ref[0]) + sh_ref[0]
        h_scr[...] = h.astype(BF16)
        acc_scr[...] = jnp.zeros_like(acc_scr)

    h = h_scr[...]
    g = jnp.dot(h, wg_ref[...], preferred_element_type=F32)
    u = jnp.dot(h, wu_ref[...], preferred_element_type=F32)
    a = (_silu(g) * u).astype(BF16)
    acc_scr[...] += jnp.dot(a, wd_ref[...], preferred_element_type=F32)

    @pl.when(j == pl.num_programs(1) - 1)
    def _():
        o_ref[...] = x_ref[...] + gt_ref[0] * _rms(acc_scr[...], gpost_ref[...])


def _ffn(x, mods, mod_row, g_pre, g_post, wg, wu, wd, tm, th):
    M, D = x.shape
    H = wg.shape[1]
    row = lambda i, j: (i, 0)
    full = lambda i, j: (0, 0)
    return pl.pallas_call(
        _ffn_kernel,
        grid=(M // tm, H // th),
        in_specs=[pl.BlockSpec((tm, D), row),
                  pl.BlockSpec((1, 1, D), lambda i, j: (mod_row(i), 0, 3)),
                  pl.BlockSpec((1, 1, D), lambda i, j: (mod_row(i), 0, 4)),
                  pl.BlockSpec((1, 1, D), lambda i, j: (mod_row(i), 0, 5)),
                  pl.BlockSpec((1, D), full),
                  pl.BlockSpec((1, D), full),
                  pl.BlockSpec((D, th), lambda i, j: (0, j)),
                  pl.BlockSpec((D, th), lambda i, j: (0, j)),
                  pl.BlockSpec((th, D), lambda i, j: (j, 0))],
        out_specs=pl.BlockSpec((tm, D), row),
        out_shape=jax.ShapeDtypeStruct((M, D), F32),
        scratch_shapes=[pltpu.VMEM((tm, D), BF16), pltpu.VMEM((tm, D), F32)],
        compiler_params=_cparams(("parallel", "arbitrary")),
        name="ffn",
    )(x, mods, mods, mods, g_pre, g_post, wg, wu, wd)


def _s5_param_kernel(lre_ref, lim_ref, ldt_ref, bre_ref, bim_ref, cre_ref, cim_ref,
                     k_ref, care_ref, ncaim_ref, abre_ref, abim_ref, alre_ref, alim_ref):
    dt = jnp.exp(ldt_ref[...])
    lre, lim = lre_ref[...], lim_ref[...]
    mag = jnp.exp(lre * dt)
    ar = mag * jnp.cos(lim * dt)
    ai = mag * jnp.sin(lim * dt)
    den = lre * lre + lim * lim
    f_re = ((ar - 1.0) * lre + ai * lim) / den
    f_im = (ai * lre - (ar - 1.0) * lim) / den
    b_re, b_im = bre_ref[...], bim_ref[...]
    bb_re = f_re * b_re - f_im * b_im
    bb_im = f_re * b_im + f_im * b_re
    c_re, c_im = cre_ref[...], cim_ref[...]

    def lag_kernel(ca, bb):
        return jnp.einsum('ghp,gjp->ghj', ca, bb, precision=lax.Precision.HIGHEST,
                          preferred_element_type=F32)

    pr, pi = jnp.ones_like(ar), jnp.zeros_like(ar)
    for k in range(S5_CHUNK + 1):
        ca_re = c_re * pr - c_im * pi
        ca_im = c_re * pi + c_im * pr
        care_ref[k] = ca_re
        ncaim_ref[k] = -ca_im
        if k < S5_CHUNK:
            k_ref[k] = lag_kernel(ca_re, bb_re) - lag_kernel(ca_im, bb_im)
            abre_ref[k] = pr * bb_re - pi * bb_im
            abim_ref[k] = pr * bb_im + pi * bb_re
        else:
            alre_ref[...] = pr
            alim_ref[...] = pi
        pr, pi = pr * ar - pi * ai, pr * ai + pi * ar


def _s5_params(lam_re, lam_im, log_dt, bt_re, bt_im, c_re, c_im):
    _, G, P = lam_re.shape
    Hg = bt_re.shape[-2]
    Lc = S5_CHUNK
    gb = 8
    par = lambda *s: pl.BlockSpec((None, gb) + s, lambda d, g: (d, g) + (0,) * len(s))
    lag = lambda n, *s: pl.BlockSpec((None, n, gb) + s, lambda d, g: (d, 0, g) + (0,) * len(s))
    return pl.pallas_call(
        _s5_param_kernel,
        grid=(2, G // gb),
        in_specs=[par(1, P), par(1, P), par(1, 1), par(Hg, P), par(Hg, P), par(Hg, P), par(Hg, P)],
        out_specs=[lag(Lc, Hg, Hg), lag(Lc + 1, Hg, P), lag(Lc + 1, Hg, P),
                   lag(Lc, Hg, P), lag(Lc, Hg, P), par(1, P), par(1, P)],
        out_shape=[jax.ShapeDtypeStruct((2, Lc, G, Hg, Hg), F32),
                   jax.ShapeDtypeStruct((2, Lc + 1, G, Hg, P), F32),
                   jax.ShapeDtypeStruct((2, Lc + 1, G, Hg, P), F32),
                   jax.ShapeDtypeStruct((2, Lc, G, Hg, P), F32),
                   jax.ShapeDtypeStruct((2, Lc, G, Hg, P), F32),
                   jax.ShapeDtypeStruct((2, G, 1, P), F32),
                   jax.ShapeDtypeStruct((2, G, 1, P), F32)],
        compiler_params=_cparams(("parallel", "parallel")),
        name="s5_params",
    )(lam_re.reshape(2, G, 1, P), lam_im.reshape(2, G, 1, P), log_dt.reshape(2, G, 1, 1),
      bt_re, bt_im, c_re, c_im)


def _s5_scan_kernel(xc_ref, xl_ref, tt_ref, ore_ref, oim_ref, wre_ref, wim_ref, al_ref,
                    yc_ref, yl_ref, u_scr, y_scr, sin_re, sin_im, sst_re, sst_im,
                    *, nb, nc_ctx, nc_lat):
    rc, rl = nb * nc_ctx, nb * nc_lat
    hg = S5_GROUP
    n_g = xc_ref.shape[-1] // hg
    for gi in range(n_g):
        lo = gi * hg
        u_scr[0:rc, :] = jnp.concatenate([xc_ref[:, t, lo:lo + hg] for t in range(S5_CHUNK)], axis=1)
        u_scr[rc:rc + rl, :] = jnp.concatenate([xl_ref[:, t, lo:lo + hg] for t in range(S5_CHUNK)],
                                               axis=1)
        u = u_scr[...].astype(BF16)
        y = None
        for d in range(2):
            sin_re[...] = jnp.dot(u, wre_ref[gi, d], preferred_element_type=F32)
            sin_im[...] = jnp.dot(u, wim_ref[gi, d], preferred_element_type=F32)
            ar = al_ref[gi, 2 * d:2 * d + 1, :]
            ai = al_ref[gi, 2 * d + 1:2 * d + 2, :]

            def make_body(base, n):
                def body(i, carry):
                    sr, si = carry
                    rows = pl.ds(base + (i if d == 0 else n - 1 - i), nb, stride=n)
                    sst_re[rows, :] = sr
                    sst_im[rows, :] = si
                    return (ar * sr - ai * si + sin_re[rows, :], ar * si + ai * sr + sin_im[rows, :])
                return body

            z = jnp.zeros((nb, S5_STATE), F32)
            carry = lax.fori_loop(0, nc_ctx, make_body(0, nc_ctx), (z, z))
            lax.fori_loop(0, nc_lat, make_body(rc, nc_lat), carry)
            yd = jnp.dot(u, tt_ref[gi, d], preferred_element_type=F32)
            yd += _bdot(sst_re[...], ore_ref[gi, d])
            yd += _bdot(sst_im[...], oim_ref[gi, d])
            y = yd if y is None else y + yd
        y_scr[gi] = y
    for t in range(S5_CHUNK):
        piece = jnp.concatenate([y_scr[gi, :, t * hg:(t + 1) * hg] for gi in range(n_g)], axis=1)
        yc_ref[:, t, :] = piece[:rc]
        yl_ref[:, t, :] = piece[rc:]


def _s5_scan(p_ctx, p_lat, B, tt, ore, oim, wre, wim, al):
    G, P, W, Lc = tt.shape[0], S5_STATE, S5_ROW, S5_CHUNK
    lanes = 128
    gb = lanes // S5_GROUP
    rc, rl = p_ctx.shape[0] // Lc, p_lat.shape[0] // Lc
    xc = p_ctx.reshape(rc, Lc, p_ctx.shape[1])
    xl = p_lat.reshape(rl, Lc, p_lat.shape[1])
    g3 = lambda *s: pl.BlockSpec((gb,) + s, lambda j: (j,) + (0,) * len(s))
    x3 = lambda r: pl.BlockSpec((r, Lc, lanes), lambda j: (0, 0, j))
    R = rc + rl
    yc, yl = pl.pallas_call(
        functools.partial(_s5_scan_kernel, nb=B, nc_ctx=rc // B, nc_lat=rl // B),
        grid=(G // gb,),
        in_specs=[x3(rc), x3(rl), g3(2, W, W), g3(2, P, W), g3(2, P, W), g3(2, W, P), g3(2, W, P),
                  g3(4, P)],
        out_specs=[x3(rc), x3(rl)],
        out_shape=[jax.ShapeDtypeStruct((rc, Lc, G * S5_GROUP), F32),
                   jax.ShapeDtypeStruct((rl, Lc, G * S5_GROUP), F32)],
        scratch_shapes=[pltpu.VMEM((R, W), F32), pltpu.VMEM((gb, R, W), F32)]
                       + [pltpu.VMEM((R, P), F32)] * 4,
        compiler_params=_cparams(("parallel",)),
        name="s5_scan",
    )(xc, xl, tt, ore, oim, wre, wim, al)
    return yc.reshape(rc * Lc, -1), yl.reshape(rl * Lc, -1)


def _s5_out_kernel(y_ref, u_ref, d_ref, w_ref, b_ref, o_ref):
    x = y_ref[...] + d_ref[...] * u_ref[...]
    y = 0.5 * x * (1.0 + jnp.tanh(math.sqrt(2.0 / math.pi) * (x + 0.044715 * (x * x * x))))
    gate = _bdot(y, w_ref[...]) + b_ref[...]
    o_ref[...] = (y * _sigmoid(gate)).astype(o_ref.dtype)


def _s5_out(y, p, d, w, b, tm):
    M, W = y.shape
    row = lambda i: (i, 0)
    full = lambda i: (0, 0)
    return pl.pallas_call(
        _s5_out_kernel,
        grid=(M // tm,),
        in_specs=[pl.BlockSpec((tm, W), row), pl.BlockSpec((tm, W), row),
                  pl.BlockSpec((1, W), full), pl.BlockSpec((W, W), full),
                  pl.BlockSpec((1, W), full)],
        out_specs=pl.BlockSpec((tm, W), row),
        out_shape=jax.ShapeDtypeStruct((M, W), BF16),
        compiler_params=_cparams(("parallel",)),
        name="s5_out",
    )(y, p, d, w, b)


def _s5_mixer(p_ctx, p_lat, B, lam_re, lam_im, log_dt, b_re, b_im, c_re, c_im, d, glu_w, glu_b):
    G, Hg, P, Lc = lam_re.shape[1], S5_GROUP, S5_STATE, S5_CHUNK
    W = G * Hg
    kk, care, ncaim, abre, abim, alre, alim = _s5_params(
        lam_re, lam_im, log_dt, b_re.swapaxes(-1, -2), b_im.swapaxes(-1, -2), c_re, c_im)
    s_i = jnp.arange(Lc)[:, None]
    t_i = jnp.arange(Lc)[None, :]

    def toeplitz(k, lag):
        m = jnp.where((lag >= 0)[:, :, None, None, None], k[jnp.clip(lag, 0, Lc - 1)], 0.0)
        return m.transpose(2, 0, 4, 1, 3).reshape(G, S5_ROW, S5_ROW)

    tt = jnp.stack([toeplitz(kk[0], t_i - s_i), toeplitz(kk[1], s_i - t_i)], 1).astype(BF16)

    def readout(ca):
        f = ca[0, 1:]
        b = ca[1, :0:-1]
        o = jnp.stack([f, b], 0)
        return o.transpose(2, 0, 4, 1, 3).reshape(G, 2, P, S5_ROW).astype(BF16)

    def writein(ab):
        w = jnp.stack([ab[0, ::-1], ab[1]], 0)
        return w.transpose(2, 0, 1, 3, 4).reshape(G, 2, S5_ROW, P).astype(BF16)

    ore, oim = readout(care), readout(ncaim)
    wre, wim = writein(abre), writein(abim)
    al = jnp.stack([alre[0], alim[0], alre[1], alim[1]], 1).reshape(G, 4, P)

    y_c, y_l = _s5_scan(p_ctx, p_lat, B, tt, ore, oim, wre, wim, al)
    dd = d.reshape(1, W)
    gw = glu_w.astype(BF16)
    gb = glu_b.reshape(1, W)
    return (_s5_out(y_c, p_ctx, dd, gw, gb, min(1024, y_c.shape[0])),
            _s5_out(y_l, p_lat, dd, gw, gb, min(1024, y_l.shape[0])))


LANES = 128
GATE_SLOT = 8


def _dot_exact_lhs(b01, a):
    a0, a1, a2 = _split3(a)
    b = b01.astype(BF16)
    d = functools.partial(jnp.dot, preferred_element_type=F32)
    return d(b, a0) + (d(b, a1) + d(b, a2))


def _rt_window(heads):
    start = (6 * heads) // SUBLANES * SUBLANES
    assert 8 * heads <= start + 2 * GATE_SLOT
    return start, 6 * heads - start


def _gates_kernel(xc_ref, xl_ref, pa_ref, pb_ref, gg_ref, gm_ref, cumt_ref, rt_ref, btm_ref,
                  *, tc, tl, heads):
    blk, H = SUPER, heads
    n2 = 2 * H
    per = blk // CHUNK
    cols_per_blk = blk // (tl // GRID_W)
    ri = lax.broadcasted_iota(jnp.int32, (blk, blk), 0)
    ci = lax.broadcasted_iota(jnp.int32, (blk, blk), 1)
    same = (ri // CHUNK) == (ci // CHUNK)
    tri_f = (same & (ri >= ci)).astype(BF16)
    tri_b = (same & (ri <= ci)).astype(BF16)
    fwd_lane = (lax.broadcasted_iota(jnp.int32, (blk, LANES), 1) % n2) < H
    src = lax.broadcasted_iota(jnp.int32, (LANES, LANES), 0)
    dst = lax.broadcasted_iota(jnp.int32, (LANES, LANES), 1)

    def perm(base, q):
        rel = src - base
        tgt = (rel % H) * (2 * GATE_SLOT) + (rel // H) * GATE_SLOT + q
        return ((rel >= 0) & (rel < n2) & (dst == tgt)).astype(BF16)

    i_to_f = ((src >= 2 * n2) & (src < 3 * n2) & (dst == src + n2)).astype(BF16)
    perms_g = [perm(n2, 0)] + [perm(0, q) for q in (1, 2, 3, 4)]
    perms_m = [perm(3 * n2, q) for q in (0, 1, 2)]
    rt0, _ = _rt_window(H)
    pa, pb = pa_ref[...], pb_ref[...]
    zpad = jnp.zeros((blk, LANES - 2 * GATE_SLOT), F32)

    def cumsums(x):
        f = _dot_exact_lhs(tri_f, x)
        b = _dot_exact_lhs(tri_b, x)
        return jnp.where(fwd_lane, f, b), f + b - x

    def place(xs, perms, o_ref):
        out = None
        for x, p in zip(xs, perms):
            t = _dot_exact_rhs(x, p)
            out = t if out is None else out + t
        for h in range(H):
            piece = out[:, h * 2 * GATE_SLOT:(h + 1) * 2 * GATE_SLOT]
            o_ref[:, h * LANES:(h + 1) * LANES] = jnp.concatenate([piece, zpad], axis=1)

    def emit(x_seq, x_scan):
        g = -jnp.exp(pa) * _softplus(x_seq + pb)
        cum, tot = cumsums(g)
        place([_sigmoid(x_seq), cum, jnp.exp(cum), jnp.exp(tot - cum), jnp.exp(tot)], perms_g, gg_ref)
        cumt_ref[...] = cum.T[0:2 * GATE_SLOT, :]
        xb = x_scan + pb
        ig = _dot_exact_rhs(xb, i_to_f)
        b, bt = cumsums(-_softplus(-xb))
        log_w = bt - b + ig
        mch = jnp.max(log_w.reshape(per, CHUNK, LANES), axis=1, keepdims=True)
        ew = jnp.exp(log_w.reshape(per, CHUNK, LANES) - mch).reshape(blk, LANES)
        r = ig - b
        place([b, ew, r], perms_m, gm_ref)
        rt_ref[...] = r.T[rt0:rt0 + 2 * GATE_SLOT, :]
        for j in range(per):
            btm_ref[j, 0:1, :] = bt[j * CHUNK:j * CHUNK + 1, :]
            btm_ref[j, 1:2, :] = mch[j]

    sc = pl.program_id(1)
    n_ctx = tc // blk

    @pl.when(sc < n_ctx)
    def _():
        x = xc_ref[pl.ds(pl.multiple_of(sc * blk, blk), blk), :]
        emit(x, x)

    @pl.when(sc >= n_ctx)
    def _():
        s = sc - n_ctx
        x_seq = xl_ref[pl.ds(pl.multiple_of(s * blk, blk), blk), :]
        x_scan = jnp.concatenate(
            [xl_ref[pl.ds(s * cols_per_blk + j, tl // GRID_W, stride=GRID_W), :]
             for j in range(cols_per_blk)], axis=0)
        emit(x_seq, x_scan)


def _gates(p_ctx, p_lat, B, gate0, pa, pb, heads):
    tc, tl = p_ctx.shape[0] // B, p_lat.shape[0] // B
    tt = tc + tl
    nsc = tt // SUPER
    per = SUPER // CHUNK
    gblk = gate0 // LANES
    tok = pl.BlockSpec((None, SUPER, heads * LANES), lambda b, s: (b, s, 0))
    rowf = pl.BlockSpec((None, None, 2 * GATE_SLOT, SUPER), lambda b, s: (b, s, 0, 0))
    return pl.pallas_call(
        functools.partial(_gates_kernel, tc=tc, tl=tl, heads=heads),
        grid=(B, nsc),
        in_specs=[pl.BlockSpec((tc, LANES), lambda b, s: (b, gblk)),
                  pl.BlockSpec((tl, LANES), lambda b, s: (b, gblk)),
                  pl.BlockSpec((1, LANES), lambda b, s: (0, 0)),
                  pl.BlockSpec((1, LANES), lambda b, s: (0, 0))],
        out_specs=[tok, tok, rowf, rowf,
                   pl.BlockSpec((None, per, 2, LANES), lambda b, s: (b, s, 0, 0))],
        out_shape=[jax.ShapeDtypeStruct((B, tt, heads * LANES), F32),
                   jax.ShapeDtypeStruct((B, tt, heads * LANES), F32),
                   jax.ShapeDtypeStruct((B, nsc, 2 * GATE_SLOT, SUPER), F32),
                   jax.ShapeDtypeStruct((B, nsc, 2 * GATE_SLOT, SUPER), F32),
                   jax.ShapeDtypeStruct((B, tt // CHUNK, 2, LANES), F32)],
        compiler_params=_cparams(("parallel", "parallel")),
        name="gates",
    )(p_ctx, p_lat, pa, pb)


def _gdn_kernel(qc_ref, kc_ref, vc_ref, zc_ref, ql_ref, kl_ref, vl_ref, zl_ref,
                cwq_ref, cwk_ref, cwv_ref, gg_ref, cumt_ref, gn_ref,
                oc_ref, ol_ref, pad_scr, qn, kn, vn, a_scr, b_scr, qp_scr, op_scr, o_scr,
                *, tc, tl, heads):
    dk = HEAD_DIM
    blk = SUPER
    head = pl.program_id(1)

    def conv(src_ref, w_ref, dst, off, t, norm):
        pad_scr[0:8, :] = jnp.zeros((8, dk), F32)
        pad_scr[8:8 + t, :] = src_ref[...]
        pad_scr[8 + t:16 + t, :] = jnp.zeros((8, dk), F32)
        w0, w1, w2 = w_ref[0:1, :], w_ref[1:2, :], w_ref[2:3, :]
        for r0 in range(0, t, blk):
            y = (w0 * pad_scr[r0 + 7:r0 + 7 + blk, :] + w1 * pad_scr[r0 + 8:r0 + 8 + blk, :]
                 + w2 * pad_scr[r0 + 9:r0 + 9 + blk, :])
            y = _silu(y)
            if norm is not None:
                y = y * lax.rsqrt(jnp.sum(y * y, axis=-1, keepdims=True) + EPS) * norm
            dst[off + r0:off + r0 + blk, :] = y

    for src_c, src_l, w_ref, dst, norm in ((qc_ref, ql_ref, cwq_ref, qn, dk ** -0.5),
                                           (kc_ref, kl_ref, cwk_ref, kn, 1.0),
                                           (vc_ref, vl_ref, cwv_ref, vn, None)):
        conv(src_c, w_ref, dst, 0, tc, norm)
        conv(src_l, w_ref, dst, tc, tl, norm)

    n_sc = (tc + tl) // blk
    ncc, ncl = tc // CHUNK, tl // CHUNK
    per = blk // CHUNK
    nt = ((1,), (1,))

    ri = lax.broadcasted_iota(jnp.int32, (blk, blk), 0)
    ci = lax.broadcasted_iota(jnp.int32, (blk, blk), 1)
    same = (ri // CHUNK) == (ci // CHUNK)
    eye_b = (ri == ci).astype(F32)
    tri_m = (same & (ri >= ci), same & (ri <= ci))
    strict_m = (same & (ri > ci), same & (ri < ci))
    lr = lax.broadcasted_iota(jnp.int32, (per * dk, blk), 0)
    lc = lax.broadcasted_iota(jnp.int32, (per * dk, blk), 1)
    chunk_sel = (lr // dk) == (lc // CHUNK)

    def prepare(scs):
        r0s = [pl.multiple_of(sc * blk, blk) for sc in scs]
        qs = [qn[pl.ds(r0, blk), :] for r0 in r0s]
        ks = [kn[pl.ds(r0, blk), :] for r0 in r0s]
        vs = [vn[pl.ds(r0, blk), :] for r0 in r0s]
        qkts = [_bdot(q, k, nt) for q, k in zip(qs, ks)]
        gts = [gg_ref[pl.ds(r0, blk), :] for r0 in r0s]
        items = [(i, d) for i in range(len(scs)) for d in (0, 1)]
        cgs = [gts[i][:, d * GATE_SLOT:(d + 1) * GATE_SLOT] for i, d in items]
        crows = [cumt_ref[scs[i], pl.ds(d * heads + head, 1), :] for i, d in items]
        decs = [jnp.where(tri_m[d], jnp.exp(jnp.where(tri_m[d], cg[:, 1:2] - crow, 0.0)), 0.0)
                for (i, d), cg, crow in zip(items, cgs, crows)]
        kbs = [ks[i] * cg[:, 0:1] for (i, d), cg in zip(items, cgs)]
        ms = [jnp.where(strict_m[d], _bdot(kb, ks[i], nt) * dec, 0.0)
              for (i, d), kb, dec in zip(items, kbs, decs)]
        ps = [eye_b - jnp.where((ri // 2) == (ci // 2), m, 0.0) for m in ms]
        n = 2
        while n < CHUNK:
            join = ((ri // (2 * n)) == (ci // (2 * n))) & ((ri // n) != (ci // n))
            ps = [p - _bdot(p, _bdot(jnp.where(join, m, 0.0), p)) for p, m in zip(ps, ms)]
            n *= 2
        uws = [_bdot(p, jnp.concatenate([vs[i] * cg[:, 0:1], kb * cg[:, 2:3]], axis=1))
               for (i, d), p, cg, kb in zip(items, ps, cgs, kbs)]
        ows = [_bdot(jnp.where(tri_m[d], qkts[i] * dec, 0.0), uw)
               for (i, d), dec, uw in zip(items, decs, uws)]
        lhss = [jnp.where(chunk_sel, jnp.concatenate([(ks[i] * cg[:, 3:4]).T] * per, axis=0), 0.0)
                for (i, d), cg in zip(items, cgs)]
        abs_ = [_bdot(lhs, uw) for lhs, uw in zip(lhss, uws)]
        for (i, d), cg, ow, ab in zip(items, cgs, ows, abs_):
            ra = pl.multiple_of(scs[i] * (per * dk), per * dk)
            qp_scr[d, pl.ds(r0s[i], blk), :] = (qs[i] * cg[:, 2:3] - ow[:, dk:]).astype(BF16)
            op_scr[d, pl.ds(r0s[i], blk), :] = ow[:, :dk]
            a_scr[d, pl.ds(ra, per * dk), :] = ab[:, dk:].astype(BF16)
            b_scr[d, pl.ds(ra, per * dk), :] = ab[:, :dk]

    def prepare_one(sc, _):
        prepare((sc,))
        return 0

    def prepare_two(i, _):
        prepare((tc // blk + 2 * i, tc // blk + 2 * i + 1))
        return 0

    n_lat = tl // blk
    lax.fori_loop(0, tc // blk, prepare_one, 0)
    lax.fori_loop(0, n_lat // 2, prepare_two, 0)
    if n_lat % 2:
        prepare_one(n_sc - 1, 0)

    def make_step(n, off_c):
        def step(i, carry):
            nxt = []
            for d in range(2):
                s = carry[d]
                c = off_c + (i if d == 0 else n - 1 - i)
                r0 = pl.multiple_of(c * CHUNK, CHUNK)
                ra = pl.multiple_of(c * dk, dk)
                sb = s.astype(BF16)
                o = jnp.dot(qp_scr[d, pl.ds(r0, CHUNK), :], sb, preferred_element_type=F32)
                o_scr[d, pl.ds(r0, CHUNK), :] = o + op_scr[d, pl.ds(r0, CHUNK), :]
                egt = gg_ref[pl.ds(r0, 1), :][:, d * GATE_SLOT + 4:d * GATE_SLOT + 5]
                nxt.append(s * egt + b_scr[d, pl.ds(ra, dk), :]
                           - jnp.dot(a_scr[d, pl.ds(ra, dk), :], sb, preferred_element_type=F32))
            return tuple(nxt)
        return step

    s0 = jnp.zeros((dk, dk), F32)
    carry = lax.fori_loop(0, ncc, make_step(ncc, 0), (s0, s0), unroll=2)
    lax.fori_loop(0, ncl, make_step(ncl, ncc), carry, unroll=2)

    gn = gn_ref[...]
    for z_ref, o_ref, off, t in ((zc_ref, oc_ref, 0, tc), (zl_ref, ol_ref, tc, tl)):
        for r0 in range(0, t, blk):
            o = o_scr[0, off + r0:off + r0 + blk, :] + o_scr[1, off + r0:off + r0 + blk, :]
            o_ref[r0:r0 + blk, :] = (_rms(o, gn) * _silu(z_ref[r0:r0 + blk, :])).astype(o_ref.dtype)


def _gdn(p_ctx, p_lat, B, conv_w, gg, cumt, gnorm, H):
    tc, tl = p_ctx.shape[0] // B, p_lat.shape[0] // B
    tt = tc + tl
    q0, k0, v0, z0 = 4, 4 + H, 4 + 2 * H, 4 + 3 * H

    def col(t, c0):
        return pl.BlockSpec((t, HEAD_DIM), lambda b, h: (b, c0 + h))

    return pl.pallas_call(
        functools.partial(_gdn_kernel, tc=tc, tl=tl, heads=H),
        grid=(B, H),
        in_specs=[col(tc, q0), col(tc, k0), col(tc, v0), col(tc, z0),
                  col(tl, q0), col(tl, k0), col(tl, v0), col(tl, z0),
                  pl.BlockSpec((3, HEAD_DIM), lambda b, h: (0, h)),
                  pl.BlockSpec((3, HEAD_DIM), lambda b, h: (0, H + h)),
                  pl.BlockSpec((3, HEAD_DIM), lambda b, h: (0, 2 * H + h)),
                  pl.BlockSpec((None, tt, LANES), lambda b, h: (b, 0, h)),
                  pl.BlockSpec((None,) + cumt.shape[1:], lambda b, h: (b, 0, 0, 0)),
                  pl.BlockSpec((1, HEAD_DIM), lambda b, h: (0, 0))],
        out_specs=[pl.BlockSpec((tc, HEAD_DIM), lambda b, h: (b, h)),
                   pl.BlockSpec((tl, HEAD_DIM), lambda b, h: (b, h))],
        out_shape=[jax.ShapeDtypeStruct((B * tc, H * HEAD_DIM), BF16),
                   jax.ShapeDtypeStruct((B * tl, H * HEAD_DIM), BF16)],
        scratch_shapes=[pltpu.VMEM((max(tc, tl) + 16, HEAD_DIM), F32)]
                       + [pltpu.VMEM((tt, HEAD_DIM), F32)] * 3
                       + [pltpu.VMEM((2, tt // CHUNK * HEAD_DIM, HEAD_DIM), BF16),
                          pltpu.VMEM((2, tt // CHUNK * HEAD_DIM, HEAD_DIM), F32),
                          pltpu.VMEM((2, tt, HEAD_DIM), BF16),
                          pltpu.VMEM((2, tt, HEAD_DIM), F32),
                          pltpu.VMEM((2, tt, HEAD_DIM), F32)],
        compiler_params=_cparams(("parallel", "parallel")),
        name="gdn",
    )(p_ctx, p_ctx, p_ctx, p_ctx, p_lat, p_lat, p_lat, p_lat, conv_w, conv_w, conv_w,
      gg, cumt, gnorm)


def _mlstm_kernel(qkc_ref, vc_ref, oc_ref, qkl_ref, vl_ref, ol_ref, gm_ref, rt_ref,
                  btm_ref, gn_ref, hc_ref, hl_ref, na_scr, kva_scr, qs_scr, ms_scr, h_scr, sc_scr,
                  *, tc, tl, heads):
    blk, dk, dv = SUPER, ML_DK, HEAD_DIM
    per = blk // CHUNK
    ncc, ncl = tc // CHUNK, tl // CHUNK
    qscale = dk ** -0.5
    head = pl.program_id(1)
    rt_off = _rt_window(heads)[1]
    grid_rows = tl // GRID_W
    cols_per_blk = blk // grid_rows

    def seq_rows(ref, s):
        return ref[pl.ds(pl.multiple_of(s * blk, blk), blk), :]

    def grid_col_rows(ref, s):
        return jnp.concatenate([ref[pl.ds(s * cols_per_blk + j, grid_rows, stride=GRID_W), :]
                                for j in range(cols_per_blk)], axis=0)

    lane_n = lax.broadcasted_iota(jnp.int32, (ncc + ncl, LANES), 1)
    for d in range(2):
        pick = lane_n == 6 * heads + d * heads + head
        for j in range(2):
            col = jnp.sum(jnp.where(pick, btm_ref[:, j, :], 0.0), axis=1, keepdims=True)
            sc_scr[d, j] = jnp.broadcast_to(col, (ncc + ncl, 2 * dv))

    ri = lax.broadcasted_iota(jnp.int32, (blk, blk), 0)
    ci = lax.broadcasted_iota(jnp.int32, (blk, blk), 1)
    same = (ri // CHUNK) == (ci // CHUNK)
    tri_m = (same & (ri >= ci), same & (ri <= ci))
    lane2 = lax.broadcasted_iota(jnp.int32, (blk, 2 * dv), 1)
    one_col = (lax.broadcasted_iota(jnp.int32, (blk, dv), 1) == 0).astype(F32)

    for d in range(2):
        def make_scalar_step(n, off_c):
            def step(i, m_st):
                c = off_c + (i if d == 0 else n - 1 - i)
                bt = sc_scr[d, 0, pl.ds(c, 1), :]
                mch = sc_scr[d, 1, pl.ds(c, 1), :]
                m_new = jnp.maximum(bt + m_st, mch)
                ms_scr[d, c, 0:1, :] = m_st
                ms_scr[d, c, 1:2, :] = jnp.exp(bt + m_st - m_new)
                ms_scr[d, c, 2:3, :] = jnp.exp(mch - m_new)
                return m_new
            return step
        m_c = lax.fori_loop(0, ncc, make_scalar_step(ncc, 0), jnp.zeros((1, 2 * dv), F32))
        lax.fori_loop(0, ncl, make_scalar_step(ncl, ncc), m_c)

    rblk = lax.broadcasted_iota(jnp.int32, (blk, 1), 0) // CHUNK

    def per_chunk_col(vals):
        col = vals[per - 1]
        for j in range(per - 2, -1, -1):
            col = jnp.where(rblk == j, vals[j], col)
        return col

    def prepare(qk_ref, v_ref, off_sc, rows, ss):
        scs = [off_sc + s for s in ss]
        r0s = [pl.multiple_of(sc * blk, blk) for sc in scs]
        qks = [rows(qk_ref, s) for s in ss]
        qs = [qk[:, :dk] * qscale for qk in qks]
        ks = [qk[:, dk:] for qk in qks]
        vas = [jnp.concatenate([rows(v_ref, s), one_col], axis=1) for s in ss]
        qkts = [_bdot(q, k, ((1,), (1,))) for q, k in zip(qs, ks)]
        cms = [gm_ref[pl.ds(r0, blk), :] for r0 in r0s]
        items = [(i, d) for i in range(len(ss)) for d in (0, 1)]
        b_cols = [cms[i][:, d * GATE_SLOT:d * GATE_SLOT + 1] for i, d in items]
        ews = [cms[i][:, d * GATE_SLOT + 1:d * GATE_SLOT + 2] for i, d in items]
        r_rows = [rt_ref[scs[i], pl.ds(rt_off + d * heads + head, 1), :] for i, d in items]
        mss = [[ms_scr[d, scs[i] * per + j] for j in range(per)] for i, d in items]
        m_sts = [per_chunk_col([m[0:1, 0:1] for m in ms]) for ms in mss]
        a_news = [per_chunk_col([m[2:3, 0:1] for m in ms]) for ms in mss]
        log_ds = [jnp.where(tri_m[d], b + r, -jnp.inf) for (i, d), b, r in zip(items, b_cols, r_rows)]
        m_ts = [jnp.maximum(b + m, jnp.max(ld, axis=-1, keepdims=True))
                for b, m, ld in zip(b_cols, m_sts, log_ds)]
        nas = [_bdot(qkts[i] * jnp.exp(ld - mt), vas[i])
               for (i, d), ld, mt in zip(items, log_ds, m_ts)]
        lhss = [jnp.where(same, jnp.concatenate([(ks[i] * (ew * an)).T] * per, axis=0), 0.0)
                for (i, d), ew, an in zip(items, ews, a_news)]
        kvas = [_bdot(lhs, vas[i]) for (i, d), lhs in zip(items, lhss)]
        for n, (i, d) in enumerate(items):
            na_scr[d, pl.ds(r0s[i], blk), :] = jnp.where(lane2 == dv + 1, jnp.exp(-m_ts[n]), nas[n])
            qs_scr[d, pl.ds(r0s[i], blk), :] = (
                qs[i] * jnp.exp(b_cols[n] + m_sts[n] - m_ts[n])).astype(BF16)
            kva_scr[d, pl.ds(r0s[i], blk), :] = kvas[n]

    def prepare_ctx(s, _):
        prepare(qkc_ref, vc_ref, 0, seq_rows, (s,))
        return 0

    def prepare_lat_pair(i, _):
        prepare(qkl_ref, vl_ref, tc // blk, grid_col_rows, (2 * i, 2 * i + 1))
        return 0

    n_lat = tl // blk
    lax.fori_loop(0, tc // blk, prepare_ctx, 0)
    lax.fori_loop(0, n_lat // 2, prepare_lat_pair, 0)
    if n_lat % 2:
        prepare(qkl_ref, vl_ref, tc // blk, grid_col_rows, (n_lat - 1,))

    def make_step(n, off_c):
        def step(i, carry):
            nxt = []
            for d in range(2):
                c_st = carry[d]
                c = off_c + (i if d == 0 else n - 1 - i)
                r0 = pl.multiple_of(c * CHUNK, CHUNK)
                na = na_scr[d, pl.ds(r0, CHUNK), :]
                tot = na + jnp.dot(qs_scr[d, pl.ds(r0, CHUNK), :], c_st.astype(BF16),
                                   preferred_element_type=F32)
                den = jnp.maximum(jnp.abs(tot[:, dv:dv + 1]), na[:, dv + 1:dv + 2])
                h_scr[d, pl.ds(r0, CHUNK), :] = tot[:, :dv] / den
                nxt.append(ms_scr[d, c, 1:2, :] * c_st + kva_scr[d, pl.ds(r0, CHUNK), :])
            return tuple(nxt)
        return step

    st0 = jnp.zeros((dk, 2 * dv), F32)
    carry = lax.fori_loop(0, ncc, make_step(ncc, 0), (st0, st0), unroll=4)
    lax.fori_loop(0, ncl, make_step(ncl, ncc), carry, unroll=4)

    gn = gn_ref[...]

    def gated(r0, og):
        h = h_scr[0, r0:r0 + blk, :] + h_scr[1, r0:r0 + blk, :]
        return (_rms(h, gn) * _sigmoid(og)).astype(hc_ref.dtype)

    for s in range(tc // blk):
        hc_ref[s * blk:(s + 1) * blk, :] = gated(s * blk, oc_ref[s * blk:(s + 1) * blk, :])
    for s in range(tl // blk):
        res = gated(tc + s * blk, grid_col_rows(ol_ref, s))
        for j in range(cols_per_blk):
            hl_ref[pl.ds(s * cols_per_blk + j, grid_rows, stride=GRID_W), :] = (
                res[j * grid_rows:(j + 1) * grid_rows])


def _mlstm(p_ctx, p_lat, B, gm, rt, btm, gnorm, qk0, H):
    tc, tl = p_ctx.shape[0] // B, p_lat.shape[0] // B
    tt = tc + tl

    def col(t, c0):
        return pl.BlockSpec((t, HEAD_DIM), lambda b, h: (b, c0 + h))

    return pl.pallas_call(
        functools.partial(_mlstm_kernel, tc=tc, tl=tl, heads=H),
        grid=(B, H),
        in_specs=[col(tc, qk0), col(tc, qk0 + H), col(tc, qk0 + 2 * H),
                  col(tl, qk0), col(tl, qk0 + H), col(tl, qk0 + 2 * H),
                  pl.BlockSpec((None, tt, LANES), lambda b, h: (b, 0, h)),
                  pl.BlockSpec((None,) + rt.shape[1:], lambda b, h: (b, 0, 0, 0)),
                  pl.BlockSpec((None,) + btm.shape[1:], lambda b, h: (b, 0, 0, 0)),
                  pl.BlockSpec((1, HEAD_DIM), lambda b, h: (0, 0))],
        out_specs=[pl.BlockSpec((tc, HEAD_DIM), lambda b, h: (b, h)),
                   pl.BlockSpec((tl, HEAD_DIM), lambda b, h: (b, h))],
        out_shape=[jax.ShapeDtypeStruct((B * tc, H * HEAD_DIM), F32),
                   jax.ShapeDtypeStruct((B * tl, H * HEAD_DIM), F32)],
        scratch_shapes=[pltpu.VMEM((2, tt, 2 * HEAD_DIM), F32)] * 2
                       + [pltpu.VMEM((2, tt, ML_DK), BF16),
                          pltpu.VMEM((2, tt // CHUNK, 3, 2 * HEAD_DIM), F32),
                          pltpu.VMEM((2, tt, HEAD_DIM), F32),
                          pltpu.VMEM((2, 2, tt // CHUNK, 2 * HEAD_DIM), F32)],
        compiler_params=_cparams(("parallel", "parallel")),
        name="mlstm",
    )(p_ctx, p_ctx, p_ctx, p_lat, p_lat, p_lat, gm, rt, btm, gnorm)


def _in_sizes(D):
    s5 = D // 4
    gw = 3 * D // 8
    gh = gw // HEAD_DIM
    mw = D - s5 - gw
    mh = mw // HEAD_DIM
    return s5, gw, gh, mw, mh


def _permute_w_in(w, D):
    s5, gw, gh, mw, mh = _in_sizes(D)
    o_a = s5 + 4 * gw
    o_mq = o_a + 4 * gh
    o_mk = o_mq + mh * ML_DK
    o_mv = o_mk + mh * ML_DK
    o_mi = o_mv + 2 * mw
    end = o_mi + 4 * mh
    pieces = [w[:, :o_a]]
    for h in range(mh):
        pieces += [w[:, o_mq + h * ML_DK:o_mq + (h + 1) * ML_DK],
                   w[:, o_mk + h * ML_DK:o_mk + (h + 1) * ML_DK]]
    pieces += [w[:, o_mv:o_mi], w[:, o_a:o_mq], w[:, o_mi:end]]
    used = o_a + 2 * mh * ML_DK + 2 * mw + 4 * gh + 4 * mh
    total = -(-used // 512) * 512
    pieces.append(jnp.zeros((w.shape[0], total - used), w.dtype))
    gate0 = o_a + 2 * mh * ML_DK + 2 * mw
    return jnp.concatenate(pieces, axis=1).astype(BF16), gate0


def _token_mixer(p_ctx, p_lat, B, gate0, prm):
    (s5_lam_re, s5_lam_im, s5_log_dt, s5_b_re, s5_b_im, s5_c_re, s5_c_im, s5_d, s5_glu_w,
     s5_glu_b, gdn_conv_w, gdn_a_log, gdn_dt_bias, gdn_norm, ml_i_bias, ml_f_bias, ml_norm) = prm
    tc, tl = p_ctx.shape[0] // B, p_lat.shape[0] // B
    rows = tl // GRID_W
    H = gdn_a_log.shape[-1]
    nch = (tc + tl) // CHUNK

    s5_c, s5_l = _s5_mixer(p_ctx, p_lat, B, s5_lam_re, s5_lam_im, s5_log_dt, s5_b_re, s5_b_im,
                           s5_c_re, s5_c_im, s5_d, s5_glu_w, s5_glu_b)

    n2 = 2 * H
    zero = jnp.zeros((n2,), F32)
    tail = jnp.zeros((LANES - 4 * n2,), F32)
    pa = jnp.concatenate([gdn_a_log.reshape(-1), zero, zero, zero, tail]).reshape(1, LANES)
    pb = jnp.concatenate([gdn_dt_bias.reshape(-1), zero, ml_i_bias.reshape(-1),
                          ml_f_bias.reshape(-1), tail]).reshape(1, LANES)
    gg, gm, cumt, rt, btm = _gates(p_ctx, p_lat, B, gate0, pa, pb, H)

    gd_c, gd_l = _gdn(p_ctx, p_lat, B, gdn_conv_w, gg, cumt, gdn_norm.reshape(1, HEAD_DIM), H)
    qk0 = gate0 // HEAD_DIM - 3 * H
    ml_c, ml_l = _mlstm(p_ctx, p_lat, B, gm, rt, btm, ml_norm.reshape(1, HEAD_DIM), qk0, H)
    return (s5_c, gd_c, ml_c), (s5_l, gd_l, ml_l)


def kernel(x, c, ctx, c_ctx, ada_w, ada_b, norm_mix_pre, norm_mix_post, norm_ffn_pre, norm_ffn_post, w_in, w_out, s5_lam_re, s5_lam_im, s5_log_dt, s5_b_re, s5_b_im, s5_c_re, s5_c_im, s5_d, s5_glu_w, s5_glu_b, gdn_conv_w, gdn_a_log, gdn_dt_bias, gdn_norm, mlstm_i_bias, mlstm_f_bias, mlstm_norm, ffn_w_gate, ffn_w_up, ffn_w_down):
    B, T, D = x.shape
    TC = ctx.shape[1]
    L = ada_w.shape[0]
    assert T % SUPER == 0 and TC % SUPER == 0 and T % GRID_W == 0 and B <= SUBLANES - 1
    s5w, gw, _, _, _ = _in_sizes(D)

    x_lat = x.reshape(B * T, D)
    x_ctx = ctx.reshape(B * TC, D)
    c8 = jnp.concatenate([c, c_ctx[None], jnp.zeros((SUBLANES - B - 1, D), F32)], 0)
    mods = _ada(c8, ada_w, ada_b).reshape(L * SUBLANES, 1, 6 * D)

    tm_lat = 1024
    tm_ctx = min(1024, B * TC)
    tiles_per_batch = T // tm_lat

    for l in range(L):
        lat_row = lambda i, l=l: l * SUBLANES + i // tiles_per_batch
        ctx_row = lambda i, l=l: l * SUBLANES + B
        g_mix_pre = norm_mix_pre[l].reshape(1, D)
        g_mix_post = norm_mix_post[l].reshape(1, D)
        g_ffn_pre = norm_ffn_pre[l].reshape(1, D)
        g_ffn_post = norm_ffn_post[l].reshape(1, D)

        w_in_p, gate0 = _permute_w_in(w_in[l], D)
        p_lat = _inproj(x_lat, mods, lat_row, g_mix_pre, w_in_p, tm_lat, 512)
        p_ctx = _inproj(x_ctx, mods, ctx_row, g_mix_pre, w_in_p, tm_ctx, 512)

        prm = (s5_lam_re[l], s5_lam_im[l], s5_log_dt[l], s5_b_re[l], s5_b_im[l], s5_c_re[l],
               s5_c_im[l], s5_d[l], s5_glu_w[l], s5_glu_b[l], gdn_conv_w[l], gdn_a_log[l],
               gdn_dt_bias[l], gdn_norm[l], mlstm_i_bias[l], mlstm_f_bias[l], mlstm_norm[l])
        mix_ctx, mix_lat = _token_mixer(p_ctx, p_lat, B, gate0, prm)

        wo = w_out[l].astype(BF16)
        wa, wb, wc = wo[:s5w], wo[s5w:s5w + gw], wo[s5w + gw:]
        wg = ffn_w_gate[l].astype(BF16)
        wu = ffn_w_up[l].astype(BF16)
        wd = ffn_w_down[l].astype(BF16)

        lat_row_o = lambda i, l=l: l * SUBLANES + i // (T // 256)
        lat_row_f = lambda i, l=l: l * SUBLANES + i // (T // 512)
        xs = _outproj(x_lat, mix_lat[0], mix_lat[1], mix_lat[2], mods, lat_row_o, g_mix_post,
                      wa, wb, wc, 256)
        x_lat = _ffn(xs, mods, lat_row_f, g_ffn_pre, g_ffn_post, wg, wu, wd, 512, 512)
        if l < L - 1:
            xs = _outproj(x_ctx, mix_ctx[0], mix_ctx[1], mix_ctx[2], mods, ctx_row, g_mix_post,
                          wa, wb, wc, 256)
            x_ctx = _ffn(xs, mods, ctx_row, g_ffn_pre, g_ffn_post, wg, wu, wd, 512, 512)
    return x_lat.reshape(B, T, D)
```

```python
import functools
import math

import jax
import jax.numpy as jnp
from jax import lax
from jax.experimental import pallas as pl
from jax.experimental.pallas import tpu as pltpu

F32 = jnp.float32
BF16 = jnp.bfloat16

EPS = 1e-6
GRID_W = 64
CHUNK = 64
SUPER = 4 * CHUNK
HEAD_DIM = 128
S5_GROUP = 16
S5_STATE = 64
S5_CHUNK = 16
S5_ROW = S5_CHUNK * S5_GROUP
SUBLANES = 8
ML_DK = 64

VMEM_LIMIT = 48 * 1024 * 1024


def _cparams(sem):
    return pltpu.CompilerParams(dimension_semantics=sem, vmem_limit_bytes=VMEM_LIMIT)


def _bdot(a, b, dims=((1,), (0,))):
    return lax.dot_general(a.astype(BF16), b.astype(BF16), (dims, ((), ())),
                           preferred_element_type=F32)


def _split3(a):
    a0 = a.astype(BF16)
    r1 = a - a0.astype(F32)
    a1 = r1.astype(BF16)
    a2 = (r1 - a1.astype(F32)).astype(BF16)
    return a0, a1, a2


def _dot3(a, b):
    a0, a1, _ = _split3(a)
    b0, b1, _ = _split3(b)
    d = functools.partial(jnp.dot, preferred_element_type=F32)
    return d(a0, b0) + (d(a0, b1) + d(a1, b0))


def _dot_exact_rhs(a, b01):
    a0, a1, a2 = _split3(a)
    b = b01.astype(BF16)
    d = functools.partial(jnp.dot, preferred_element_type=F32)
    return d(a0, b) + (d(a1, b) + d(a2, b))


def _sigmoid(x):
    return 1.0 / (1.0 + jnp.exp(-x))


def _silu(x):
    return x * _sigmoid(x)


def _softplus(x):
    return jnp.maximum(x, 0.0) + jnp.log(1.0 + jnp.exp(-jnp.abs(x)))


def _rms(x, g):
    return x * lax.rsqrt(jnp.mean(x * x, axis=-1, keepdims=True) + EPS) * g


def _ada_kernel(c_ref, w_ref, b_ref, o_ref):
    c = c_ref[...]
    o_ref[...] = _bdot(_silu(c), w_ref[...]) + b_ref[...]


def _ada(c8, ada_w, ada_b):
    L, D, N = ada_w.shape
    tn = 1024
    return pl.pallas_call(
        _ada_kernel,
        grid=(L, N // tn),
        in_specs=[pl.BlockSpec((SUBLANES, D), lambda l, j: (0, 0)),
                  pl.BlockSpec((None, D, tn), lambda l, j: (l, 0, j)),
                  pl.BlockSpec((None, 1, tn), lambda l, j: (l, 0, j))],
        out_specs=pl.BlockSpec((None, SUBLANES, tn), lambda l, j: (l, 0, j)),
        out_shape=jax.ShapeDtypeStruct((L, SUBLANES, N), F32),
        compiler_params=_cparams(("parallel", "arbitrary")),
        name="ada",
    )(c8, ada_w, ada_b.reshape(L, 1, N))


def _inproj_kernel(x_ref, sh_ref, sc_ref, g_ref, w_ref, o_ref, h_scr):
    @pl.when(pl.program_id(1) == 0)
    def _():
        h = _rms(x_ref[...], g_ref[...]) * (1.0 + sc_ref[0]) + sh_ref[0]
        h_scr[...] = h.astype(BF16)

    o_ref[...] = jnp.dot(h_scr[...], w_ref[...], preferred_element_type=F32)


def _inproj(x, mods, mod_row, g_pre, w, tm, tn):
    M, D = x.shape
    N = w.shape[1]
    return pl.pallas_call(
        _inproj_kernel,
        grid=(M // tm, N // tn),
        in_specs=[pl.BlockSpec((tm, D), lambda i, j: (i, 0)),
                  pl.BlockSpec((1, 1, D), lambda i, j: (mod_row(i), 0, 0)),
                  pl.BlockSpec((1, 1, D), lambda i, j: (mod_row(i), 0, 1)),
                  pl.BlockSpec((1, D), lambda i, j: (0, 0)),
                  pl.BlockSpec((D, tn), lambda i, j: (0, j))],
        out_specs=pl.BlockSpec((tm, tn), lambda i, j: (i, j)),
        out_shape=jax.ShapeDtypeStruct((M, N), F32),
        scratch_shapes=[pltpu.VMEM((tm, D), BF16)],
        compiler_params=_cparams(("parallel", "arbitrary")),
        name="inproj",
    )(x, mods, mods, g_pre, w)


def _outproj_kernel(x_ref, a_ref, b_ref, c_ref, gt_ref, g_ref, wa_ref, wb_ref, wc_ref, o_ref):
    acc = jnp.dot(a_ref[...], wa_ref[...], preferred_element_type=F32)
    acc += jnp.dot(b_ref[...], wb_ref[...], preferred_element_type=F32)
    acc += jnp.dot(c_ref[...].astype(BF16), wc_ref[...], preferred_element_type=F32)
    o_ref[...] = x_ref[...] + gt_ref[0] * _rms(acc, g_ref[...])


def _outproj(x, ma, mb, mc, mods, mod_row, g_post, wa, wb, wc, tm):
    M, D = x.shape
    row = lambda i: (i, 0)
    full = lambda i: (0, 0)
    return pl.pallas_call(
        _outproj_kernel,
        grid=(M // tm,),
        in_specs=[pl.BlockSpec((tm, D), row),
                  pl.BlockSpec((tm, ma.shape[1]), row),
                  pl.BlockSpec((tm, mb.shape[1]), row),
                  pl.BlockSpec((tm, mc.shape[1]), row),
                  pl.BlockSpec((1, 1, D), lambda i: (mod_row(i), 0, 2)),
                  pl.BlockSpec((1, D), full),
                  pl.BlockSpec(wa.shape, full),
                  pl.BlockSpec(wb.shape, full),
                  pl.BlockSpec(wc.shape, full)],
        out_specs=pl.BlockSpec((tm, D), row),
        out_shape=jax.ShapeDtypeStruct((M, D), F32),
        compiler_params=_cparams(("parallel",)),
        name="outproj",
    )(x, ma, mb, mc, mods, g_post, wa, wb, wc)


def _ffn_kernel(x_ref, sh_ref, sc_ref, gt_ref, gpre_ref, gpost_ref, wg_ref, wu_ref, wd_ref,
                o_ref, h_scr, acc_scr):
    j = pl.program_id(1)

    @pl.when(j == 0)
    def _():
        h = _rms(x_ref[...], gpre_ref[...]) * (1.0 + sc_ref[0]) + sh_ref[0]
        h_scr[...] = h.astype(BF16)
        acc_scr[...] = jnp.zeros_like(acc_scr)

    h = h_scr[...]
    g = jnp.dot(h, wg_ref[...], preferred_element_type=F32)
    u = jnp.dot(h, wu_ref[...], preferred_element_type=F32)
    a = (_silu(g) * u).astype(BF16)
    acc_scr[...] += jnp.dot(a, wd_ref[...], preferred_element_type=F32)

    @pl.when(j == pl.num_programs(1) - 1)
    def _():
        o_ref[...] = x_ref[...] + gt_ref[0] * _rms(acc_scr[...], gpost_ref[...])


def _ffn(x, mods, mod_row, g_pre, g_post, wg, wu, wd, tm, th):
    M, D = x.shape
    H = wg.shape[1]
    row = lambda i, j: (i, 0)
    full = lambda i, j: (0, 0)
    return pl.pallas_call(
        _ffn_kernel,
        grid=(M // tm, H // th),
        in_specs=[pl.BlockSpec((tm, D), row),
                  pl.BlockSpec((1, 1, D), lambda i, j: (mod_row(i), 0, 3)),
                  pl.BlockSpec((1, 1, D), lambda i, j: (mod_row(i), 0, 4)),
                  pl.BlockSpec((1, 1, D), lambda i, j: (mod_row(i), 0, 5)),
                  pl.BlockSpec((1, D), full),
                  pl.BlockSpec((1, D), full),
                  pl.BlockSpec((D, th), lambda i, j: (0, j)),
                  pl.BlockSpec((D, th), lambda i, j: (0, j)),
                  pl.BlockSpec((th, D), lambda i, j: (j, 0))],
        out_specs=pl.BlockSpec((tm, D), row),
        out_shape=jax.ShapeDtypeStruct((M, D), F32),
        scratch_shapes=[pltpu.VMEM((tm, D), BF16), pltpu.VMEM((tm, D), F32)],
        compiler_params=_cparams(("parallel", "arbitrary")),
        name="ffn",
    )(x, mods, mods, mods, g_pre, g_post, wg, wu, wd)


def _s5_param_kernel(lre_ref, lim_ref, ldt_ref, bre_ref, bim_ref, cre_ref, cim_ref,
                     k_ref, ca_ref, ab_ref, alre_ref, alim_ref):
    dt = jnp.exp(ldt_ref[...])
    lre, lim = lre_ref[...], lim_ref[...]
    mag = jnp.exp(lre * dt)
    ar = mag * jnp.cos(lim * dt)
    ai = mag * jnp.sin(lim * dt)
    den = lre * lre + lim * lim
    f_re = ((ar - 1.0) * lre + ai * lim) / den
    f_im = (ai * lre - (ar - 1.0) * lim) / den
    b_re, b_im = bre_ref[...], bim_ref[...]
    bb_re = f_re * b_re - f_im * b_im
    bb_im = f_re * b_im + f_im * b_re
    c_re, c_im = cre_ref[...], cim_ref[...]

    def lag_kernel(ca, bb):
        return jnp.einsum('gjp,ghp->gjh', bb, ca, precision=lax.Precision.HIGHEST,
                          preferred_element_type=F32)

    pr, pi = jnp.ones_like(ar), jnp.zeros_like(ar)
    kts = []
    for k in range(S5_CHUNK + 1):
        ca_re = c_re * pr - c_im * pi
        ca_im = c_re * pi + c_im * pr
        ca_ref[k] = jnp.concatenate([ca_re, -ca_im], axis=-1)
        if k < S5_CHUNK:
            kts.append(lag_kernel(ca_re, bb_re) - lag_kernel(ca_im, bb_im))
            ab_ref[k] = jnp.concatenate([pr * bb_re - pi * bb_im, pr * bb_im + pi * bb_re], axis=-1)
        else:
            alre_ref[...] = pr
            alim_ref[...] = pi
        pr, pi = pr * ar - pi * ai, pr * ai + pi * ar
    k_ref[...] = jnp.concatenate(kts, axis=-1)


def _s5_params(lam_re, lam_im, log_dt, bt_re, bt_im, c_re, c_im):
    _, G, P = lam_re.shape
    Hg = bt_re.shape[-2]
    Lc = S5_CHUNK
    gb = 8
    par = lambda *s: pl.BlockSpec((None, gb) + s, lambda d, g: (d, g) + (0,) * len(s))
    lag = lambda n, *s: pl.BlockSpec((None, n, gb) + s, lambda d, g: (d, 0, g) + (0,) * len(s))
    return pl.pallas_call(
        _s5_param_kernel,
        grid=(2, G // gb),
        in_specs=[par(1, P), par(1, P), par(1, 1), par(Hg, P), par(Hg, P), par(Hg, P), par(Hg, P)],
        out_specs=[par(Hg, Lc * Hg), lag(Lc + 1, Hg, 2 * P), lag(Lc, Hg, 2 * P), par(1, P), par(1, P)],
        out_shape=[jax.ShapeDtypeStruct((2, G, Hg, Lc * Hg), F32),
                   jax.ShapeDtypeStruct((2, Lc + 1, G, Hg, 2 * P), F32),
                   jax.ShapeDtypeStruct((2, Lc, G, Hg, 2 * P), F32),
                   jax.ShapeDtypeStruct((2, G, 1, P), F32),
                   jax.ShapeDtypeStruct((2, G, 1, P), F32)],
        compiler_params=_cparams(("parallel", "parallel")),
        name="s5_params",
    )(lam_re.reshape(2, G, 1, P), lam_im.reshape(2, G, 1, P), log_dt.reshape(2, G, 1, 1),
      bt_re, bt_im, c_re, c_im)


def _s5_scan_kernel(xc_ref, xl_ref, kt_ref, ca_ref, ab_ref, alre_ref, alim_ref,
                    yc_ref, yl_ref, u_scr, y_scr, sin_re, sin_im, sst_re, sst_im,
                    *, nb, nc_ctx, nc_lat):
    rc, rl = nb * nc_ctx, nb * nc_lat
    hg, Lc, P = S5_GROUP, S5_CHUNK, S5_STATE
    n_g = xc_ref.shape[-1] // hg
    zero_blk = jnp.zeros((hg, hg), F32)
    for gi in range(n_g):
        lo = gi * hg
        u_scr[0:rc, :] = jnp.concatenate([xc_ref[:, t, lo:lo + hg] for t in range(Lc)], axis=1)
        u_scr[rc:rc + rl, :] = jnp.concatenate([xl_ref[:, t, lo:lo + hg] for t in range(Lc)], axis=1)
        u = u_scr[...].astype(BF16)
        for d in range(2):
            w = jnp.concatenate([ab_ref[d, Lc - 1 - s if d == 0 else s, gi] for s in range(Lc)], axis=0)
            sin = jnp.dot(u, w.astype(BF16), preferred_element_type=F32)
            sin_re[d] = sin[:, :P]
            sin_im[d] = sin[:, P:]
        als = [(alre_ref[d, gi], alim_ref[d, gi]) for d in range(2)]

        def make_body(base, n):
            def body(i, carry):
                nxt = []
                for d in range(2):
                    sr, si = carry[d]
                    ar, ai = als[d]
                    rows = pl.ds(base + (i if d == 0 else n - 1 - i), nb, stride=n)
                    sst_re[d, rows, :] = sr
                    sst_im[d, rows, :] = si
                    nxt.append((ar * sr - ai * si + sin_re[d, rows, :],
                                ar * si + ai * sr + sin_im[d, rows, :]))
                return tuple(nxt)
            return body

        z = jnp.zeros((nb, P), F32)
        carry = lax.fori_loop(0, nc_ctx, make_body(0, nc_ctx), ((z, z), (z, z)))
        lax.fori_loop(0, nc_lat, make_body(rc, nc_lat), carry)
        y = None
        for d in range(2):
            kt = kt_ref[d, gi]
            rows_tt = []
            for s in range(Lc):
                lag = [(t - s) if d == 0 else (s - t) for t in range(Lc)]
                rows_tt.append(jnp.concatenate(
                    [kt[:, k * hg:(k + 1) * hg] if k >= 0 else zero_blk for k in lag], axis=1))
            tt = jnp.concatenate(rows_tt, axis=0)
            o = jnp.concatenate([ca_ref[d, t + 1 if d == 0 else Lc - t, gi] for t in range(Lc)], axis=0)
            sst = jnp.concatenate([sst_re[d], sst_im[d]], axis=1)
            yd = jnp.dot(u, tt.astype(BF16), preferred_element_type=F32)
            yd += _bdot(sst, o, ((1,), (1,)))
            y = yd if y is None else y + yd
        y_scr[gi] = y
    for t in range(Lc):
        piece = jnp.concatenate([y_scr[gi, :, t * hg:(t + 1) * hg] for gi in range(n_g)], axis=1)
        yc_ref[:, t, :] = piece[:rc]
        yl_ref[:, t, :] = piece[rc:]


def _s5_scan(p_ctx, p_lat, B, kt, ca, ab, alre, alim):
    G, P, W, Lc = kt.shape[1], S5_STATE, S5_ROW, S5_CHUNK
    gb = LANES // S5_GROUP
    rc, rl = p_ctx.shape[0] // Lc, p_lat.shape[0] // Lc
    xc = p_ctx.reshape(rc, Lc, p_ctx.shape[1])
    xl = p_lat.reshape(rl, Lc, p_lat.shape[1])
    par = lambda *s: pl.BlockSpec((2, gb) + s, lambda j: (0, j) + (0,) * len(s))
    lag = lambda n, *s: pl.BlockSpec((2, n, gb) + s, lambda j: (0, 0, j) + (0,) * len(s))
    x3 = lambda r: pl.BlockSpec((r, Lc, LANES), lambda j: (0, 0, j))
    R = rc + rl
    yc, yl = pl.pallas_call(
        functools.partial(_s5_scan_kernel, nb=B, nc_ctx=rc // B, nc_lat=rl // B),
        grid=(G // gb,),
        in_specs=[x3(rc), x3(rl), par(S5_GROUP, Lc * S5_GROUP), lag(Lc + 1, S5_GROUP, 2 * P),
                  lag(Lc, S5_GROUP, 2 * P), par(1, P), par(1, P)],
        out_specs=[x3(rc), x3(rl)],
        out_shape=[jax.ShapeDtypeStruct((rc, Lc, G * S5_GROUP), F32),
                   jax.ShapeDtypeStruct((rl, Lc, G * S5_GROUP), F32)],
        scratch_shapes=[pltpu.VMEM((R, W), F32), pltpu.VMEM((gb, R, W), F32)]
                       + [pltpu.VMEM((2, R, P), F32)] * 4,
        compiler_params=_cparams(("parallel",)),
        name="s5_scan",
    )(xc, xl, kt, ca, ab, alre, alim)
    return yc.reshape(rc * Lc, -1), yl.reshape(rl * Lc, -1)


def _s5_out_kernel(y_ref, u_ref, d_ref, w_ref, b_ref, o_ref):
    x = y_ref[...] + d_ref[...] * u_ref[...]
    y = 0.5 * x * (1.0 + jnp.tanh(math.sqrt(2.0 / math.pi) * (x + 0.044715 * (x * x * x))))
    gate = _bdot(y, w_ref[...]) + b_ref[...]
    o_ref[...] = (y * _sigmoid(gate)).astype(o_ref.dtype)


def _s5_out(y, p, d, w, b, tm):
    M, W = y.shape
    row = lambda i: (i, 0)
    full = lambda i: (0, 0)
    return pl.pallas_call(
        _s5_out_kernel,
        grid=(M // tm,),
        in_specs=[pl.BlockSpec((tm, W), row), pl.BlockSpec((tm, W), row),
                  pl.BlockSpec((1, W), full), pl.BlockSpec((W, W), full),
                  pl.BlockSpec((1, W), full)],
        out_specs=pl.BlockSpec((tm, W), row),
        out_shape=jax.ShapeDtypeStruct((M, W), BF16),
        compiler_params=_cparams(("parallel",)),
        name="s5_out",
    )(y, p, d, w, b)


def _s5_mixer(p_ctx, p_lat, B, lam_re, lam_im, log_dt, b_re, b_im, c_re, c_im, d, glu_w, glu_b):
    G, Hg, P, Lc = lam_re.shape[1], S5_GROUP, S5_STATE, S5_CHUNK
    W = G * Hg
    kt, ca, ab, alre, alim = _s5_params(
        lam_re, lam_im, log_dt, b_re.swapaxes(-1, -2), b_im.swapaxes(-1, -2), c_re, c_im)
    y_c, y_l = _s5_scan(p_ctx, p_lat, B, kt, ca, ab, alre, alim)
    dd = d.reshape(1, W)
    gw = glu_w.astype(BF16)
    gb = glu_b.reshape(1, W)
    return (_s5_out(y_c, p_ctx, dd, gw, gb, min(1024, y_c.shape[0])),
            _s5_out(y_l, p_lat, dd, gw, gb, min(1024, y_l.shape[0])))


LANES = 128
GATE_SLOT = 8


def _dot_exact_lhs(b01, a):
    a0, a1, a2 = _split3(a)
    b = b01.astype(BF16)
    d = functools.partial(jnp.dot, preferred_element_type=F32)
    return d(b, a0) + (d(b, a1) + d(b, a2))


def _rt_window(heads):
    start = (6 * heads) // SUBLANES * SUBLANES
    assert 8 * heads <= start + 2 * GATE_SLOT
    return start, 6 * heads - start


def _gates_kernel(xc_ref, xl_ref, pa_ref, pb_ref, gg_ref, gm_ref, cumt_ref, rt_ref, btm_ref,
                  *, tc, tl, heads):
    blk, H = SUPER, heads
    n2 = 2 * H
    per = blk // CHUNK
    cols_per_blk = blk // (tl // GRID_W)
    ri = lax.broadcasted_iota(jnp.int32, (blk, blk), 0)
    ci = lax.broadcasted_iota(jnp.int32, (blk, blk), 1)
    same = (ri // CHUNK) == (ci // CHUNK)
    tri_f = (same & (ri >= ci)).astype(BF16)
    tri_b = (same & (ri <= ci)).astype(BF16)
    fwd_lane = (lax.broadcasted_iota(jnp.int32, (blk, LANES), 1) % n2) < H
    src = lax.broadcasted_iota(jnp.int32, (LANES, LANES), 0)
    dst = lax.broadcasted_iota(jnp.int32, (LANES, LANES), 1)

    def perm(base, q):
        rel = src - base
        tgt = (rel % H) * (2 * GATE_SLOT) + (rel // H) * GATE_SLOT + q
        return ((rel >= 0) & (rel < n2) & (dst == tgt)).astype(BF16)

    i_to_f = ((src >= 2 * n2) & (src < 3 * n2) & (dst == src + n2)).astype(BF16)
    perms_g = [perm(n2, 0)] + [perm(0, q) for q in (1, 2, 3, 4)]
    perms_m = [perm(3 * n2, q) for q in (0, 1, 2)]
    rt0, _ = _rt_window(H)
    pa, pb = pa_ref[...], pb_ref[...]
    zpad = jnp.zeros((blk, LANES - 2 * GATE_SLOT), F32)

    def cumsums(x):
        f = _dot_exact_lhs(tri_f, x)
        b = _dot_exact_lhs(tri_b, x)
        return jnp.where(fwd_lane, f, b), f + b - x

    def place(xs, perms, o_ref):
        out = None
        for x, p in zip(xs, perms):
            t = _dot_exact_rhs(x, p)
            out = t if out is None else out + t
        for h in range(H):
            piece = out[:, h * 2 * GATE_SLOT:(h + 1) * 2 * GATE_SLOT]
            o_ref[:, h * LANES:(h + 1) * LANES] = jnp.concatenate([piece, zpad], axis=1)

    def emit(x_seq, x_scan):
        g = -jnp.exp(pa) * _softplus(x_seq + pb)
        cum, tot = cumsums(g)
        place([_sigmoid(x_seq), cum, jnp.exp(cum), jnp.exp(tot - cum), jnp.exp(tot)], perms_g, gg_ref)
        cumt_ref[...] = cum.T[0:2 * GATE_SLOT, :]
        xb = x_scan + pb
        ig = _dot_exact_rhs(xb, i_to_f)
        b, bt = cumsums(-_softplus(-xb))
        log_w = bt - b + ig
        mch = jnp.max(log_w.reshape(per, CHUNK, LANES), axis=1, keepdims=True)
        ew = jnp.exp(log_w.reshape(per, CHUNK, LANES) - mch).reshape(blk, LANES)
        r = ig - b
        place([b, ew, r], perms_m, gm_ref)
        rt_ref[...] = r.T[rt0:rt0 + 2 * GATE_SLOT, :]
        for j in range(per):
            btm_ref[j, 0:1, :] = bt[j * CHUNK:j * CHUNK + 1, :]
            btm_ref[j, 1:2, :] = mch[j]

    sc = pl.program_id(1)
    n_ctx = tc // blk

    @pl.when(sc < n_ctx)
    def _():
        x = xc_ref[pl.ds(pl.multiple_of(sc * blk, blk), blk), :]
        emit(x, x)

    @pl.when(sc >= n_ctx)
    def _():
        s = sc - n_ctx
        x_seq = xl_ref[pl.ds(pl.multiple_of(s * blk, blk), blk), :]
        x_scan = jnp.concatenate(
            [xl_ref[pl.ds(s * cols_per_blk + j, tl // GRID_W, stride=GRID_W), :]
             for j in range(cols_per_blk)], axis=0)
        emit(x_seq, x_scan)


def _gates(p_ctx, p_lat, B, gate0, pa, pb, heads):
    tc, tl = p_ctx.shape[0] // B, p_lat.shape[0] // B
    tt = tc + tl
    nsc = tt // SUPER
    per = SUPER // CHUNK
    gblk = gate0 // LANES
    tok = pl.BlockSpec((None, SUPER, heads * LANES), lambda b, s: (b, s, 0))
    rowf = pl.BlockSpec((None, None, 2 * GATE_SLOT, SUPER), lambda b, s: (b, s, 0, 0))
    return pl.pallas_call(
        functools.partial(_gates_kernel, tc=tc, tl=tl, heads=heads),
        grid=(B, nsc),
        in_specs=[pl.BlockSpec((tc, LANES), lambda b, s: (b, gblk)),
                  pl.BlockSpec((tl, LANES), lambda b, s: (b, gblk)),
                  pl.BlockSpec((1, LANES), lambda b, s: (0, 0)),
                  pl.BlockSpec((1, LANES), lambda b, s: (0, 0))],
        out_specs=[tok, tok, rowf, rowf,
                   pl.BlockSpec((None, per, 2, LANES), lambda b, s: (b, s, 0, 0))],
        out_shape=[jax.ShapeDtypeStruct((B, tt, heads * LANES), F32),
                   jax.ShapeDtypeStruct((B, tt, heads * LANES), F32),
                   jax.ShapeDtypeStruct((B, nsc, 2 * GATE_SLOT, SUPER), F32),
                   jax.ShapeDtypeStruct((B, nsc, 2 * GATE_SLOT, SUPER), F32),
                   jax.ShapeDtypeStruct((B, tt // CHUNK, 2, LANES), F32)],
        compiler_params=_cparams(("parallel", "parallel")),
        name="gates",
    )(p_ctx, p_lat, pa, pb)


def _gdn_kernel(qc_ref, kc_ref, vc_ref, zc_ref, ql_ref, kl_ref, vl_ref, zl_ref,
                cwq_ref, cwk_ref, cwv_ref, gg_ref, cumt_ref, gn_ref,
                oc_ref, ol_ref, pad_scr, qn, kn, vn, a_scr, b_scr, qp_scr, op_scr, o_scr,
                a2_scr, b2_scr, *, tc, tl, heads):
    dk = HEAD_DIM
    blk = SUPER
    head = pl.program_id(1)

    def conv(src_ref, w_ref, dst, off, t, norm):
        pad_scr[0:8, :] = jnp.zeros((8, dk), F32)
        pad_scr[8:8 + t, :] = src_ref[...]
        pad_scr[8 + t:16 + t, :] = jnp.zeros((8, dk), F32)
        w0, w1, w2 = w_ref[0:1, :], w_ref[1:2, :], w_ref[2:3, :]
        for r0 in range(0, t, blk):
            y = (w0 * pad_scr[r0 + 7:r0 + 7 + blk, :] + w1 * pad_scr[r0 + 8:r0 + 8 + blk, :]
                 + w2 * pad_scr[r0 + 9:r0 + 9 + blk, :])
            y = _silu(y)
            if norm is not None:
                y = y * lax.rsqrt(jnp.sum(y * y, axis=-1, keepdims=True) + EPS) * norm
            dst[off + r0:off + r0 + blk, :] = y

    for src_c, src_l, w_ref, dst, norm in ((qc_ref, ql_ref, cwq_ref, qn, dk ** -0.5),
                                           (kc_ref, kl_ref, cwk_ref, kn, 1.0),
                                           (vc_ref, vl_ref, cwv_ref, vn, None)):
        conv(src_c, w_ref, dst, 0, tc, norm)
        conv(src_l, w_ref, dst, tc, tl, norm)

    n_sc = (tc + tl) // blk
    ncc, ncl = tc // CHUNK, tl // CHUNK
    per = blk // CHUNK
    nt = ((1,), (1,))

    ri = lax.broadcasted_iota(jnp.int32, (blk, blk), 0)
    ci = lax.broadcasted_iota(jnp.int32, (blk, blk), 1)
    same = (ri // CHUNK) == (ci // CHUNK)
    eye_b = (ri == ci).astype(F32)
    tri_m = (same & (ri >= ci), same & (ri <= ci))
    strict_m = (same & (ri > ci), same & (ri < ci))
    lr = lax.broadcasted_iota(jnp.int32, (per * dk, blk), 0)
    lc = lax.broadcasted_iota(jnp.int32, (per * dk, blk), 1)
    chunk_sel = (lr // dk) == (lc // CHUNK)

    def prepare(scs):
        r0s = [pl.multiple_of(sc * blk, blk) for sc in scs]
        qs = [qn[pl.ds(r0, blk), :] for r0 in r0s]
        ks = [kn[pl.ds(r0, blk), :] for r0 in r0s]
        vs = [vn[pl.ds(r0, blk), :] for r0 in r0s]
        qkts = [_bdot(q, k, nt) for q, k in zip(qs, ks)]
        gts = [gg_ref[pl.ds(r0, blk), :] for r0 in r0s]
        items = [(i, d) for i in range(len(scs)) for d in (0, 1)]
        cgs = [gts[i][:, d * GATE_SLOT:(d + 1) * GATE_SLOT] for i, d in items]
        crows = [cumt_ref[scs[i], pl.ds(d * heads + head, 1), :] for i, d in items]
        decs = [jnp.where(tri_m[d], jnp.exp(jnp.where(tri_m[d], cg[:, 1:2] - crow, 0.0)), 0.0)
                for (i, d), cg, crow in zip(items, cgs, crows)]
        kbs = [ks[i] * cg[:, 0:1] for (i, d), cg in zip(items, cgs)]
        ms = [jnp.where(strict_m[d], _bdot(kb, ks[i], nt) * dec, 0.0)
              for (i, d), kb, dec in zip(items, kbs, decs)]
        ps = [eye_b - jnp.where((ri // 2) == (ci // 2), m, 0.0) for m in ms]
        n = 2
        while n < CHUNK:
            join = ((ri // (2 * n)) == (ci // (2 * n))) & ((ri // n) != (ci // n))
            ps = [p - _bdot(p, _bdot(jnp.where(join, m, 0.0), p)) for p, m in zip(ps, ms)]
            n *= 2
        uws = [_bdot(p, jnp.concatenate([vs[i] * cg[:, 0:1], kb * cg[:, 2:3]], axis=1))
               for (i, d), p, cg, kb in zip(items, ps, cgs, kbs)]
        ows = [_bdot(jnp.where(tri_m[d], qkts[i] * dec, 0.0), uw)
               for (i, d), dec, uw in zip(items, decs, uws)]
        lhss = [jnp.where(chunk_sel, jnp.concatenate([(ks[i] * cg[:, 3:4]).T] * per, axis=0), 0.0)
                for (i, d), cg in zip(items, cgs)]
        abs_ = [_bdot(lhs, uw) for lhs, uw in zip(lhss, uws)]
        for (i, d), cg, ow, ab in zip(items, cgs, ows, abs_):
            ra = pl.multiple_of(scs[i] * (per * dk), per * dk)
            qp_scr[d, pl.ds(r0s[i], blk), :] = (qs[i] * cg[:, 2:3] - ow[:, dk:]).astype(BF16)
            op_scr[d, pl.ds(r0s[i], blk), :] = ow[:, :dk]
            a_scr[d, pl.ds(ra, per * dk), :] = ab[:, dk:].astype(BF16)
            b_scr[d, pl.ds(ra, per * dk), :] = ab[:, :dk]
            for m in range(per // 2):
                j0, j1 = (2 * m, 2 * m + 1) if d == 0 else (2 * m + 1, 2 * m)
                e0 = cg[j0 * CHUNK:j0 * CHUNK + 1, 4:5]
                e1 = cg[j1 * CHUNK:j1 * CHUNK + 1, 4:5]
                ab0 = ab[j0 * dk:(j0 + 1) * dk, :]
                ab1 = ab[j1 * dk:(j1 + 1) * dk, :]
                x = _bdot(ab1[:, dk:], ab0)
                rp = pl.multiple_of((scs[i] * (per // 2) + m) * dk, dk)
                a2_scr[d, pl.ds(rp, dk), :] = (e1 * ab0[:, dk:] + e0 * ab1[:, dk:] - x[:, dk:]).astype(BF16)
                b2_scr[d, pl.ds(rp, dk), :] = e1 * ab0[:, :dk] + ab1[:, :dk] - x[:, :dk]

    def prepare_one(sc, _):
        prepare((sc,))
        return 0

    def prepare_two(i, _):
        prepare((tc // blk + 2 * i, tc // blk + 2 * i + 1))
        return 0

    n_lat = tl // blk
    lax.fori_loop(0, tc // blk, prepare_one, 0)
    lax.fori_loop(0, n_lat // 2, prepare_two, 0)
    if n_lat % 2:
        prepare_one(n_sc - 1, 0)

    def make_step(n, off_p):
        def step(i, carry):
            nxt = []
            for d in range(2):
                s = carry[d]
                p = off_p + (i if d == 0 else n - 1 - i)
                sb = s.astype(BF16)

                def emit(c, state_bf16):
                    r0 = pl.multiple_of(c * CHUNK, CHUNK)
                    o = jnp.dot(qp_scr[d, pl.ds(r0, CHUNK), :], state_bf16, preferred_element_type=F32)
                    o_scr[d, pl.ds(r0, CHUNK), :] = o + op_scr[d, pl.ds(r0, CHUNK), :]
                    return gg_ref[pl.ds(r0, 1), :][:, d * GATE_SLOT + 4:d * GATE_SLOT + 5]

                c0, c1 = 2 * p + d, 2 * p + 1 - d
                e0 = emit(c0, sb)
                ra = pl.multiple_of(c0 * dk, dk)
                s1 = (s * e0 + b_scr[d, pl.ds(ra, dk), :]
                      - jnp.dot(a_scr[d, pl.ds(ra, dk), :], sb, preferred_element_type=F32))
                e1 = emit(c1, s1.astype(BF16))
                rp = pl.multiple_of(p * dk, dk)
                nxt.append(s * (e0 * e1) + b2_scr[d, pl.ds(rp, dk), :]
                           - jnp.dot(a2_scr[d, pl.ds(rp, dk), :], sb, preferred_element_type=F32))
            return tuple(nxt)
        return step

    s0 = jnp.zeros((dk, dk), F32)
    carry = lax.fori_loop(0, ncc // 2, make_step(ncc // 2, 0), (s0, s0), unroll=2)
    lax.fori_loop(0, ncl // 2, make_step(ncl // 2, ncc // 2), carry, unroll=2)

    gn = gn_ref[...]
    for z_ref, o_ref, off, t in ((zc_ref, oc_ref, 0, tc), (zl_ref, ol_ref, tc, tl)):
        for r0 in range(0, t, blk):
            o = o_scr[0, off + r0:off + r0 + blk, :] + o_scr[1, off + r0:off + r0 + blk, :]
            o_ref[r0:r0 + blk, :] = (_rms(o, gn) * _silu(z_ref[r0:r0 + blk, :])).astype(o_ref.dtype)


def _gdn(p_ctx, p_lat, B, conv_w, gg, cumt, gnorm, H):
    tc, tl = p_ctx.shape[0] // B, p_lat.shape[0] // B
    tt = tc + tl
    q0, k0, v0, z0 = 4, 4 + H, 4 + 2 * H, 4 + 3 * H

    def col(t, c0):
        return pl.BlockSpec((t, HEAD_DIM), lambda b, h: (b, c0 + h))

    return pl.pallas_call(
        functools.partial(_gdn_kernel, tc=tc, tl=tl, heads=H),
        grid=(B, H),
        in_specs=[col(tc, q0), col(tc, k0), col(tc, v0), col(tc, z0),
                  col(tl, q0), col(tl, k0), col(tl, v0), col(tl, z0),
                  pl.BlockSpec((3, HEAD_DIM), lambda b, h: (0, h)),
                  pl.BlockSpec((3, HEAD_DIM), lambda b, h: (0, H + h)),
                  pl.BlockSpec((3, HEAD_DIM), lambda b, h: (0, 2 * H + h)),
                  pl.BlockSpec((None, tt, LANES), lambda b, h: (b, 0, h)),
                  pl.BlockSpec((None,) + cumt.shape[1:], lambda b, h: (b, 0, 0, 0)),
                  pl.BlockSpec((1, HEAD_DIM), lambda b, h: (0, 0))],
        out_specs=[pl.BlockSpec((tc, HEAD_DIM), lambda b, h: (b, h)),
                   pl.BlockSpec((tl, HEAD_DIM), lambda b, h: (b, h))],
        out_shape=[jax.ShapeDtypeStruct((B * tc, H * HEAD_DIM), BF16),
                   jax.ShapeDtypeStruct((B * tl, H * HEAD_DIM), BF16)],
        scratch_shapes=[pltpu.VMEM((max(tc, tl) + 16, HEAD_DIM), F32)]
                       + [pltpu.VMEM((tt, HEAD_DIM), F32)] * 3
                       + [pltpu.VMEM((2, tt // CHUNK * HEAD_DIM, HEAD_DIM), BF16),
                          pltpu.VMEM((2, tt // CHUNK * HEAD_DIM, HEAD_DIM), F32),
                          pltpu.VMEM((2, tt, HEAD_DIM), BF16),
                          pltpu.VMEM((2, tt, HEAD_DIM), F32),
                          pltpu.VMEM((2, tt, HEAD_DIM), F32),
                          pltpu.VMEM((2, tt // (2 * CHUNK) * HEAD_DIM, HEAD_DIM), BF16),
                          pltpu.VMEM((2, tt // (2 * CHUNK) * HEAD_DIM, HEAD_DIM), F32)],
        compiler_params=_cparams(("parallel", "parallel")),
        name="gdn",
    )(p_ctx, p_ctx, p_ctx, p_ctx, p_lat, p_lat, p_lat, p_lat, conv_w, conv_w, conv_w,
      gg, cumt, gnorm)


def _mlstm_kernel(qkc_ref, vc_ref, oc_ref, qkl_ref, vl_ref, ol_ref, gm_ref, rt_ref,
                  btm_ref, gn_ref, hc_ref, hl_ref, na_scr, kva_scr, qs_scr, ms_scr, h_scr, sc_scr,
                  *, tc, tl, heads):
    blk, dk, dv = SUPER, ML_DK, HEAD_DIM
    per = blk // CHUNK
    ncc, ncl = tc // CHUNK, tl // CHUNK
    qscale = dk ** -0.5
    head = pl.program_id(1)
    rt_off = _rt_window(heads)[1]
    grid_rows = tl // GRID_W
    cols_per_blk = blk // grid_rows

    def seq_rows(ref, s):
        return ref[pl.ds(pl.multiple_of(s * blk, blk), blk), :]

    def grid_col_rows(ref, s):
        return jnp.concatenate([ref[pl.ds(s * cols_per_blk + j, grid_rows, stride=GRID_W), :]
                                for j in range(cols_per_blk)], axis=0)

    lane_n = lax.broadcasted_iota(jnp.int32, (ncc + ncl, LANES), 1)
    for d in range(2):
        pick = lane_n == 6 * heads + d * heads + head
        for j in range(2):
            col = jnp.sum(jnp.where(pick, btm_ref[:, j, :], 0.0), axis=1, keepdims=True)
            sc_scr[d, j] = jnp.broadcast_to(col, (ncc + ncl, 2 * dv))

    ri = lax.broadcasted_iota(jnp.int32, (blk, blk), 0)
    ci = lax.broadcasted_iota(jnp.int32, (blk, blk), 1)
    same = (ri // CHUNK) == (ci // CHUNK)
    tri_m = (same & (ri >= ci), same & (ri <= ci))
    lane2 = lax.broadcasted_iota(jnp.int32, (blk, 2 * dv), 1)
    one_col = (lax.broadcasted_iota(jnp.int32, (blk, dv), 1) == 0).astype(F32)

    for d in range(2):
        def make_scalar_step(n, off_c):
            def step(i, m_st):
                c = off_c + (i if d == 0 else n - 1 - i)
                bt = sc_scr[d, 0, pl.ds(c, 1), :]
                mch = sc_scr[d, 1, pl.ds(c, 1), :]
                m_new = jnp.maximum(bt + m_st, mch)
                ms_scr[d, c, 0:1, :] = m_st
                ms_scr[d, c, 1:2, :] = jnp.exp(bt + m_st - m_new)
                ms_scr[d, c, 2:3, :] = jnp.exp(mch - m_new)
                return m_new
            return step
        m_c = lax.fori_loop(0, ncc, make_scalar_step(ncc, 0), jnp.zeros((1, 2 * dv), F32))
        lax.fori_loop(0, ncl, make_scalar_step(ncl, ncc), m_c)

    rblk = lax.broadcasted_iota(jnp.int32, (blk, 1), 0) // CHUNK

    def per_chunk_col(vals):
        col = vals[per - 1]
        for j in range(per - 2, -1, -1):
            col = jnp.where(rblk == j, vals[j], col)
        return col

    def prepare(qk_ref, v_ref, off_sc, rows, ss):
        scs = [off_sc + s for s in ss]
        r0s = [pl.multiple_of(sc * blk, blk) for sc in scs]
        qks = [rows(qk_ref, s) for s in ss]
        qs = [qk[:, :dk] * qscale for qk in qks]
        ks = [qk[:, dk:] for qk in qks]
        vas = [jnp.concatenate([rows(v_ref, s), one_col], axis=1) for s in ss]
        qkts = [_bdot(q, k, ((1,), (1,))) for q, k in zip(qs, ks)]
        cms = [gm_ref[pl.ds(r0, blk), :] for r0 in r0s]
        items = [(i, d) for i in range(len(ss)) for d in (0, 1)]
        b_cols = [cms[i][:, d * GATE_SLOT:d * GATE_SLOT + 1] for i, d in items]
        ews = [cms[i][:, d * GATE_SLOT + 1:d * GATE_SLOT + 2] for i, d in items]
        r_rows = [rt_ref[scs[i], pl.ds(rt_off + d * heads + head, 1), :] for i, d in items]
        mss = [[ms_scr[d, scs[i] * per + j] for j in range(per)] for i, d in items]
        m_sts = [per_chunk_col([m[0:1, 0:1] for m in ms]) for ms in mss]
        a_news = [per_chunk_col([m[2:3, 0:1] for m in ms]) for ms in mss]
        log_ds = [jnp.where(tri_m[d], b + r, -jnp.inf) for (i, d), b, r in zip(items, b_cols, r_rows)]
        m_ts = [jnp.maximum(b + m, jnp.max(ld, axis=-1, keepdims=True))
                for b, m, ld in zip(b_cols, m_sts, log_ds)]
        nas = [_bdot(qkts[i] * jnp.exp(ld - mt), vas[i])
               for (i, d), ld, mt in zip(items, log_ds, m_ts)]
        lhss = [jnp.where(same, jnp.concatenate([(ks[i] * (ew * an)).T] * per, axis=0), 0.0)
                for (i, d), ew, an in zip(items, ews, a_news)]
        kvas = [_bdot(lhs, vas[i]) for (i, d), lhs in zip(items, lhss)]
        for n, (i, d) in enumerate(items):
            na_scr[d, pl.ds(r0s[i], blk), :] = jnp.where(lane2 == dv + 1, jnp.exp(-m_ts[n]), nas[n])
            qs_scr[d, pl.ds(r0s[i], blk), :] = (
                qs[i] * jnp.exp(b_cols[n] + m_sts[n] - m_ts[n])).astype(BF16)
            kva_scr[d, pl.ds(r0s[i], blk), :] = kvas[n]

    def prepare_ctx(s, _):
        prepare(qkc_ref, vc_ref, 0, seq_rows, (s,))
        return 0

    def prepare_lat_pair(i, _):
        prepare(qkl_ref, vl_ref, tc // blk, grid_col_rows, (2 * i, 2 * i + 1))
        return 0

    n_lat = tl // blk
    lax.fori_loop(0, tc // blk, prepare_ctx, 0)
    lax.fori_loop(0, n_lat // 2, prepare_lat_pair, 0)
    if n_lat % 2:
        prepare(qkl_ref, vl_ref, tc // blk, grid_col_rows, (n_lat - 1,))

    def make_step(n, off_c):
        def step(i, carry):
            nxt = []
            for d in range(2):
                c_st = carry[d]
                c = off_c + (i if d == 0 else n - 1 - i)
                r0 = pl.multiple_of(c * CHUNK, CHUNK)
                na = na_scr[d, pl.ds(r0, CHUNK), :]
                tot = na + jnp.dot(qs_scr[d, pl.ds(r0, CHUNK), :], c_st.astype(BF16),
                                   preferred_element_type=F32)
                den = jnp.maximum(jnp.abs(tot[:, dv:dv + 1]), na[:, dv + 1:dv + 2])
                h_scr[d, pl.ds(r0, CHUNK), :] = tot[:, :dv] / den
                nxt.append(ms_scr[d, c, 1:2, :] * c_st + kva_scr[d, pl.ds(r0, CHUNK), :])
            return tuple(nxt)
        return step

    st0 = jnp.zeros((dk, 2 * dv), F32)
    carry = lax.fori_loop(0, ncc, make_step(ncc, 0), (st0, st0), unroll=4)
    lax.fori_loop(0, ncl, make_step(ncl, ncc), carry, unroll=4)

    gn = gn_ref[...]

    def gated(r0, og):
        h = h_scr[0, r0:r0 + blk, :] + h_scr[1, r0:r0 + blk, :]
        return (_rms(h, gn) * _sigmoid(og)).astype(hc_ref.dtype)

    for s in range(tc // blk):
        hc_ref[s * blk:(s + 1) * blk, :] = gated(s * blk, oc_ref[s * blk:(s + 1) * blk, :])
    for s in range(tl // blk):
        res = gated(tc + s * blk, grid_col_rows(ol_ref, s))
        for j in range(cols_per_blk):
            hl_ref[pl.ds(s * cols_per_blk + j, grid_rows, stride=GRID_W), :] = (
                res[j * grid_rows:(j + 1) * grid_rows])


def _mlstm(p_ctx, p_lat, B, gm, rt, btm, gnorm, qk0, H):
    tc, tl = p_ctx.shape[0] // B, p_lat.shape[0] // B
    tt = tc + tl

    def col(t, c0):
        return pl.BlockSpec((t, HEAD_DIM), lambda b, h: (b, c0 + h))

    return pl.pallas_call(
        functools.partial(_mlstm_kernel, tc=tc, tl=tl, heads=H),
        grid=(B, H),
        in_specs=[col(tc, qk0), col(tc, qk0 + H), col(tc, qk0 + 2 * H),
                  col(tl, qk0), col(tl, qk0 + H), col(tl, qk0 + 2 * H),
                  pl.BlockSpec((None, tt, LANES), lambda b, h: (b, 0, h)),
                  pl.BlockSpec((None,) + rt.shape[1:], lambda b, h: (b, 0, 0, 0)),
                  pl.BlockSpec((None,) + btm.shape[1:], lambda b, h: (b, 0, 0, 0)),
                  pl.BlockSpec((1, HEAD_DIM), lambda b, h: (0, 0))],
        out_specs=[pl.BlockSpec((tc, HEAD_DIM), lambda b, h: (b, h)),
                   pl.BlockSpec((tl, HEAD_DIM), lambda b, h: (b, h))],
        out_shape=[jax.ShapeDtypeStruct((B * tc, H * HEAD_DIM), F32),
                   jax.ShapeDtypeStruct((B * tl, H * HEAD_DIM), F32)],
        scratch_shapes=[pltpu.VMEM((2, tt, 2 * HEAD_DIM), F32)] * 2
                       + [pltpu.VMEM((2, tt, ML_DK), BF16),
                          pltpu.VMEM((2, tt // CHUNK, 3, 2 * HEAD_DIM), F32),
                          pltpu.VMEM((2, tt, HEAD_DIM), F32),
                          pltpu.VMEM((2, 2, tt // CHUNK, 2 * HEAD_DIM), F32)],
        compiler_params=_cparams(("parallel", "parallel")),
        name="mlstm",
    )(p_ctx, p_ctx, p_ctx, p_lat, p_lat, p_lat, gm, rt, btm, gnorm)


def _in_sizes(D):
    s5 = D // 4
    gw = 3 * D // 8
    gh = gw // HEAD_DIM
    mw = D - s5 - gw
    mh = mw // HEAD_DIM
    return s5, gw, gh, mw, mh


def _permute_w_in(w, D):
    s5, gw, gh, mw, mh = _in_sizes(D)
    o_a = s5 + 4 * gw
    o_mq = o_a + 4 * gh
    o_mk = o_mq + mh * ML_DK
    o_mv = o_mk + mh * ML_DK
    o_mi = o_mv + 2 * mw
    end = o_mi + 4 * mh
    pieces = [w[:, :o_a]]
    for h in range(mh):
        pieces += [w[:, o_mq + h * ML_DK:o_mq + (h + 1) * ML_DK],
                   w[:, o_mk + h * ML_DK:o_mk + (h + 1) * ML_DK]]
    pieces += [w[:, o_mv:o_mi], w[:, o_a:o_mq], w[:, o_mi:end]]
    used = o_a + 2 * mh * ML_DK + 2 * mw + 4 * gh + 4 * mh
    total = -(-used // 512) * 512
    pieces.append(jnp.zeros((w.shape[0], total - used), w.dtype))
    gate0 = o_a + 2 * mh * ML_DK + 2 * mw
    return jnp.concatenate(pieces, axis=1).astype(BF16), gate0


def _token_mixer(p_ctx, p_lat, B, gate0, prm):
    (s5_lam_re, s5_lam_im, s5_log_dt, s5_b_re, s5_b_im, s5_c_re, s5_c_im, s5_d, s5_glu_w,
     s5_glu_b, gdn_conv_w, gdn_a_log, gdn_dt_bias, gdn_norm, ml_i_bias, ml_f_bias, ml_norm) = prm
    tc, tl = p_ctx.shape[0] // B, p_lat.shape[0] // B
    rows = tl // GRID_W
    H = gdn_a_log.shape[-1]
    nch = (tc + tl) // CHUNK

    s5_c, s5_l = _s5_mixer(p_ctx, p_lat, B, s5_lam_re, s5_lam_im, s5_log_dt, s5_b_re, s5_b_im,
                           s5_c_re, s5_c_im, s5_d, s5_glu_w, s5_glu_b)

    n2 = 2 * H
    zero = jnp.zeros((n2,), F32)
    tail = jnp.zeros((LANES - 4 * n2,), F32)
    pa = jnp.concatenate([gdn_a_log.reshape(-1), zero, zero, zero, tail]).reshape(1, LANES)
    pb = jnp.concatenate([gdn_dt_bias.reshape(-1), zero, ml_i_bias.reshape(-1),
                          ml_f_bias.reshape(-1), tail]).reshape(1, LANES)
    gg, gm, cumt, rt, btm = _gates(p_ctx, p_lat, B, gate0, pa, pb, H)

    gd_c, gd_l = _gdn(p_ctx, p_lat, B, gdn_conv_w, gg, cumt, gdn_norm.reshape(1, HEAD_DIM), H)
    qk0 = gate0 // HEAD_DIM - 3 * H
    ml_c, ml_l = _mlstm(p_ctx, p_lat, B, gm, rt, btm, ml_norm.reshape(1, HEAD_DIM), qk0, H)
    return (s5_c, gd_c, ml_c), (s5_l, gd_l, ml_l)


def kernel(x, c, ctx, c_ctx, ada_w, ada_b, norm_mix_pre, norm_mix_post, norm_ffn_pre, norm_ffn_post, w_in, w_out, s5_lam_re, s5_lam_im, s5_log_dt, s5_b_re, s5_b_im, s5_c_re, s5_c_im, s5_d, s5_glu_w, s5_glu_b, gdn_conv_w, gdn_a_log, gdn_dt_bias, gdn_norm, mlstm_i_bias, mlstm_f_bias, mlstm_norm, ffn_w_gate, ffn_w_up, ffn_w_down):
    B, T, D = x.shape
    TC = ctx.shape[1]
    L = ada_w.shape[0]
    assert T % SUPER == 0 and TC % SUPER == 0 and T % GRID_W == 0 and B <= SUBLANES - 1
    s5w, gw, _, _, _ = _in_sizes(D)

    x_lat = x.reshape(B * T, D)
    x_ctx = ctx.reshape(B * TC, D)
    c8 = jnp.concatenate([c, c_ctx[None], jnp.zeros((SUBLANES - B - 1, D), F32)], 0)
    mods = _ada(c8, ada_w, ada_b).reshape(L * SUBLANES, 1, 6 * D)

    tm_lat = 1024
    tm_ctx = min(1024, B * TC)
    tiles_per_batch = T // tm_lat

    for l in range(L):
        lat_row = lambda i, l=l: l * SUBLANES + i // tiles_per_batch
        ctx_row = lambda i, l=l: l * SUBLANES + B
        g_mix_pre = norm_mix_pre[l].reshape(1, D)
        g_mix_post = norm_mix_post[l].reshape(1, D)
        g_ffn_pre = norm_ffn_pre[l].reshape(1, D)
        g_ffn_post = norm_ffn_post[l].reshape(1, D)

        w_in_p, gate0 = _permute_w_in(w_in[l], D)
        p_lat = _inproj(x_lat, mods, lat_row, g_mix_pre, w_in_p, tm_lat, 512)
        p_ctx = _inproj(x_ctx, mods, ctx_row, g_mix_pre, w_in_p, tm_ctx, 512)

        prm = (s5_lam_re[l], s5_lam_im[l], s5_log_dt[l], s5_b_re[l], s5_b_im[l], s5_c_re[l],
               s5_c_im[l], s5_d[l], s5_glu_w[l], s5_glu_b[l], gdn_conv_w[l], gdn_a_log[l],
               gdn_dt_bias[l], gdn_norm[l], mlstm_i_bias[l], mlstm_f_bias[l], mlstm_norm[l])
        mix_ctx, mix_lat = _token_mixer(p_ctx, p_lat, B, gate0, prm)

        wo = w_out[l].astype(BF16)
        wa, wb, wc = wo[:s5w], wo[s5w:s5w + gw], wo[s5w + gw:]
        wg = ffn_w_gate[l].astype(BF16)
        wu = ffn_w_up[l].astype(BF16)
        wd = ffn_w_down[l].astype(BF16)

        lat_row_o = lambda i, l=l: l * SUBLANES + i // (T // 256)
        lat_row_f = lambda i, l=l: l * SUBLANES + i // (T // 512)
        xs = _outproj(x_lat, mix_lat[0], mix_lat[1], mix_lat[2], mods, lat_row_o, g_mix_post,
                      wa, wb, wc, 256)
        x_lat = _ffn(xs, mods, lat_row_f, g_ffn_pre, g_ffn_post, wg, wu, wd, 512, 512)
        if l < L - 1:
            xs = _outproj(x_ctx, mix_ctx[0], mix_ctx[1], mix_ctx[2], mods, ctx_row, g_mix_post,
                          wa, wb, wc, 256)
            x_ctx = _ffn(xs, mods, ctx_row, g_ffn_pre, g_ffn_post, wg, wu, wd, 512, 512)
    return x_lat.reshape(B, T, D)
```

```python
import functools
import math

import jax
import jax.numpy as jnp
from jax import lax
from jax.experimental import pallas as pl
from jax.experimental.pallas import tpu as pltpu

F32 = jnp.float32
BF16 = jnp.bfloat16

EPS = 1e-6
GRID_W = 64
CHUNK = 64
SUPER = 4 * CHUNK
HEAD_DIM = 128
S5_GROUP = 16
S5_STATE = 64
S5_CHUNK = 16
S5_ROW = S5_CHUNK * S5_GROUP
SUBLANES = 8
ML_DK = 64

VMEM_LIMIT = 48 * 1024 * 1024


def _cparams(sem):
    return pltpu.CompilerParams(dimension_semantics=sem, vmem_limit_bytes=VMEM_LIMIT)


def _bdot(a, b, dims=((1,), (0,))):
    return lax.dot_general(a.astype(BF16), b.astype(BF16), (dims, ((), ())),
                           preferred_element_type=F32)


def _split3(a):
    a0 = a.astype(BF16)
    r1 = a - a0.astype(F32)
    a1 = r1.astype(BF16)
    a2 = (r1 - a1.astype(F32)).astype(BF16)
    return a0, a1, a2


def _dot3(a, b):
    a0, a1, _ = _split3(a)
    b0, b1, _ = _split3(b)
    d = functools.partial(jnp.dot, preferred_element_type=F32)
    return d(a0, b0) + (d(a0, b1) + d(a1, b0))


def _dot_exact_rhs(a, b01):
    a0, a1, a2 = _split3(a)
    b = b01.astype(BF16)
    d = functools.partial(jnp.dot, preferred_element_type=F32)
    return d(a0, b) + (d(a1, b) + d(a2, b))


def _sigmoid(x):
    return 1.0 / (1.0 + jnp.exp(-x))


def _silu(x):
    return x * _sigmoid(x)


def _softplus(x):
    return jnp.maximum(x, 0.0) + jnp.log(1.0 + jnp.exp(-jnp.abs(x)))


def _rms(x, g):
    return x * lax.rsqrt(jnp.mean(x * x, axis=-1, keepdims=True) + EPS) * g


def _ada_kernel(c_ref, w_ref, b_ref, o_ref):
    c = c_ref[...]
    o_ref[...] = _bdot(_silu(c), w_ref[...]) + b_ref[...]


def _ada(c8, ada_w, ada_b):
    L, D, N = ada_w.shape
    tn = 1024
    return pl.pallas_call(
        _ada_kernel,
        grid=(L, N // tn),
        in_specs=[pl.BlockSpec((SUBLANES, D), lambda l, j: (0, 0)),
                  pl.BlockSpec((None, D, tn), lambda l, j: (l, 0, j)),
                  pl.BlockSpec((None, 1, tn), lambda l, j: (l, 0, j))],
        out_specs=pl.BlockSpec((None, SUBLANES, tn), lambda l, j: (l, 0, j)),
        out_shape=jax.ShapeDtypeStruct((L, SUBLANES, N), F32),
        compiler_params=_cparams(("parallel", "arbitrary")),
        name="ada",
    )(c8, ada_w, ada_b.reshape(L, 1, N))


def _inproj_kernel(x_ref, sh_ref, sc_ref, g_ref, w_ref, o_ref, h_scr):
    @pl.when(pl.program_id(1) == 0)
    def _():
        h = _rms(x_ref[...], g_ref[...]) * (1.0 + sc_ref[0]) + sh_ref[0]
        h_scr[...] = h.astype(BF16)

    o_ref[...] = jnp.dot(h_scr[...], w_ref[...], preferred_element_type=F32)


def _inproj(x, mods, mod_row, g_pre, w, tm, tn):
    M, D = x.shape
    N = w.shape[1]
    return pl.pallas_call(
        _inproj_kernel,
        grid=(M // tm, N // tn),
        in_specs=[pl.BlockSpec((tm, D), lambda i, j: (i, 0)),
                  pl.BlockSpec((1, 1, D), lambda i, j: (mod_row(i), 0, 0)),
                  pl.BlockSpec((1, 1, D), lambda i, j: (mod_row(i), 0, 1)),
                  pl.BlockSpec((1, D), lambda i, j: (0, 0)),
                  pl.BlockSpec((D, tn), lambda i, j: (0, j))],
        out_specs=pl.BlockSpec((tm, tn), lambda i, j: (i, j)),
        out_shape=jax.ShapeDtypeStruct((M, N), F32),
        scratch_shapes=[pltpu.VMEM((tm, D), BF16)],
        compiler_params=_cparams(("parallel", "arbitrary")),
        name="inproj",
    )(x, mods, mods, g_pre, w)


def _outproj_kernel(x_ref, a_ref, b_ref, c_ref, gt_ref, g_ref, wa_ref, wb_ref, wc_ref, o_ref):
    acc = jnp.dot(a_ref[...], wa_ref[...], preferred_element_type=F32)
    acc += jnp.dot(b_ref[...], wb_ref[...], preferred_element_type=F32)
    acc += jnp.dot(c_ref[...].astype(BF16), wc_ref[...], preferred_element_type=F32)
    o_ref[...] = x_ref[...] + gt_ref[0] * _rms(acc, g_ref[...])


def _outproj(x, ma, mb, mc, mods, mod_row, g_post, wa, wb, wc, tm):
    M, D = x.shape
    row = lambda i: (i, 0)
    full = lambda i: (0, 0)
    return pl.pallas_call(
        _outproj_kernel,
        grid=(M // tm,),
        in_specs=[pl.BlockSpec((tm, D), row),
                  pl.BlockSpec((tm, ma.shape[1]), row),
                  pl.BlockSpec((tm, mb.shape[1]), row),
                  pl.BlockSpec((tm, mc.shape[1]), row),
                  pl.BlockSpec((1, 1, D), lambda i: (mod_row(i), 0, 2)),
                  pl.BlockSpec((1, D), full),
                  pl.BlockSpec(wa.shape, full),
                  pl.BlockSpec(wb.shape, full),
                  pl.BlockSpec(wc.shape, full)],
        out_specs=pl.BlockSpec((tm, D), row),
        out_shape=jax.ShapeDtypeStruct((M, D), F32),
        compiler_params=_cparams(("parallel",)),
        name="outproj",
    )(x, ma, mb, mc, mods, g_post, wa, wb, wc)


def _ffn_kernel(x_ref, sh_ref, sc_ref, gt_ref, gpre_ref, gpost_ref, wg_ref, wu_ref, wd_ref,
                o_ref, h_scr, acc_scr):
    j = pl.program_id(1)

    @pl.when(j == 0)
    def _():
        h = _rms(x_ref[...], gpre_ref[...]) * (1.0 + sc_ref[0]) + sh_ref[0]
        h_scr[...] = h.astype(BF16)
        acc_scr[...] = jnp.zeros_like(acc_scr)

    h = h_scr[...]
    g = jnp.dot(h, wg_ref[...], preferred_element_type=F32)
    u = jnp.dot(h, wu_ref[...], preferred_element_type=F32)
    a = (_silu(g) * u).astype(BF16)
    acc_scr[...] += jnp.dot(a, wd_ref[...], preferred_element_type=F32)

    @pl.when(j == pl.num_programs(1) - 1)
    def _():
        o_ref[...] = x_ref[...] + gt_ref[0] * _rms(acc_scr[...], gpost_ref[...])


def _ffn(x, mods, mod_row, g_pre, g_post, wg, wu, wd, tm, th):
    M, D = x.shape
    H = wg.shape[1]
    row = lambda i, j: (i, 0)
    full = lambda i, j: (0, 0)
    return pl.pallas_call(
        _ffn_kernel,
        grid=(M // tm, H // th),
        in_specs=[pl.BlockSpec((tm, D), row),
                  pl.BlockSpec((1, 1, D), lambda i, j: (mod_row(i), 0, 3)),
                  pl.BlockSpec((1, 1, D), lambda i, j: (mod_row(i), 0, 4)),
                  pl.BlockSpec((1, 1, D), lambda i, j: (mod_row(i), 0, 5)),
                  pl.BlockSpec((1, D), full),
                  pl.BlockSpec((1, D), full),
                  pl.BlockSpec((D, th), lambda i, j: (0, j)),
                  pl.BlockSpec((D, th), lambda i, j: (0, j)),
                  pl.BlockSpec((th, D), lambda i, j: (j, 0))],
        out_specs=pl.BlockSpec((tm, D), row),
        out_shape=jax.ShapeDtypeStruct((M, D), F32),
        scratch_shapes=[pltpu.VMEM((tm, D), BF16), pltpu.VMEM((tm, D), F32)],
        compiler_params=_cparams(("parallel", "arbitrary")),
        name="ffn",
    )(x, mods, mods, mods, g_pre, g_post, wg, wu, wd)


def _s5_param_kernel(lre_ref, lim_ref, ldt_ref, bre_ref, bim_ref, cre_ref, cim_ref,
                     k_ref, ca_ref, ab_ref, alre_ref, alim_ref):
    dt = jnp.exp(ldt_ref[...])
    lre, lim = lre_ref[...], lim_ref[...]
    mag = jnp.exp(lre * dt)
    ar = mag * jnp.cos(lim * dt)
    ai = mag * jnp.sin(lim * dt)
    den = lre * lre + lim * lim
    f_re = ((ar - 1.0) * lre + ai * lim) / den
    f_im = (ai * lre - (ar - 1.0) * lim) / den
    b_re, b_im = bre_ref[...], bim_ref[...]
    bb_re = f_re * b_re - f_im * b_im
    bb_im = f_re * b_im + f_im * b_re
    c_re, c_im = cre_ref[...], cim_ref[...]

    def lag_kernel(ca, bb):
        return jnp.einsum('gjp,ghp->gjh', bb, ca, precision=lax.Precision.HIGHEST,
                          preferred_element_type=F32)

    pr, pi = jnp.ones_like(ar), jnp.zeros_like(ar)
    kts = []
    for k in range(S5_CHUNK + 1):
        ca_re = c_re * pr - c_im * pi
        ca_im = c_re * pi + c_im * pr
        ca_ref[k] = jnp.concatenate([ca_re, -ca_im], axis=-1)
        if k < S5_CHUNK:
            kts.append(lag_kernel(ca_re, bb_re) - lag_kernel(ca_im, bb_im))
            ab_ref[k] = jnp.concatenate([pr * bb_re - pi * bb_im, pr * bb_im + pi * bb_re], axis=-1)
        else:
            alre_ref[...] = pr
            alim_ref[...] = pi
        pr, pi = pr * ar - pi * ai, pr * ai + pi * ar
    k_ref[...] = jnp.concatenate(kts, axis=-1)


def _s5_params(lam_re, lam_im, log_dt, bt_re, bt_im, c_re, c_im):
    _, G, P = lam_re.shape
    Hg = bt_re.shape[-2]
    Lc = S5_CHUNK
    gb = 8
    par = lambda *s: pl.BlockSpec((None, gb) + s, lambda d, g: (d, g) + (0,) * len(s))
    lag = lambda n, *s: pl.BlockSpec((None, n, gb) + s, lambda d, g: (d, 0, g) + (0,) * len(s))
    return pl.pallas_call(
        _s5_param_kernel,
        grid=(2, G // gb),
        in_specs=[par(1, P), par(1, P), par(1, 1), par(Hg, P), par(Hg, P), par(Hg, P), par(Hg, P)],
        out_specs=[par(Hg, Lc * Hg), lag(Lc + 1, Hg, 2 * P), lag(Lc, Hg, 2 * P), par(1, P), par(1, P)],
        out_shape=[jax.ShapeDtypeStruct((2, G, Hg, Lc * Hg), F32),
                   jax.ShapeDtypeStruct((2, Lc + 1, G, Hg, 2 * P), F32),
                   jax.ShapeDtypeStruct((2, Lc, G, Hg, 2 * P), F32),
                   jax.ShapeDtypeStruct((2, G, 1, P), F32),
                   jax.ShapeDtypeStruct((2, G, 1, P), F32)],
        compiler_params=_cparams(("parallel", "parallel")),
        name="s5_params",
    )(lam_re.reshape(2, G, 1, P), lam_im.reshape(2, G, 1, P), log_dt.reshape(2, G, 1, 1),
      bt_re, bt_im, c_re, c_im)


def _s5_scan_kernel(xc_ref, xl_ref, kt_ref, ca_ref, ab_ref, alre_ref, alim_ref,
                    yc_ref, yl_ref, u_scr, y_scr, sin_re, sin_im, sst_re, sst_im,
                    *, nb, nc_ctx, nc_lat):
    rc, rl = nb * nc_ctx, nb * nc_lat
    hg, Lc, P = S5_GROUP, S5_CHUNK, S5_STATE
    n_g = xc_ref.shape[-1] // hg
    zero_blk = jnp.zeros((hg, hg), F32)
    for gi in range(n_g):
        lo = gi * hg
        u_scr[0:rc, :] = jnp.concatenate([xc_ref[:, t, lo:lo + hg] for t in range(Lc)], axis=1)
        u_scr[rc:rc + rl, :] = jnp.concatenate([xl_ref[:, t, lo:lo + hg] for t in range(Lc)], axis=1)
        u = u_scr[...].astype(BF16)
        for d in range(2):
            w = jnp.concatenate([ab_ref[d, Lc - 1 - s if d == 0 else s, gi] for s in range(Lc)], axis=0)
            sin = jnp.dot(u, w.astype(BF16), preferred_element_type=F32)
            sin_re[d] = sin[:, :P]
            sin_im[d] = sin[:, P:]
        als = [(alre_ref[d, gi], alim_ref[d, gi]) for d in range(2)]

        def make_body(base, n):
            def body(i, carry):
                nxt = []
                for d in range(2):
                    sr, si = carry[d]
                    ar, ai = als[d]
                    rows = pl.ds(base + (i if d == 0 else n - 1 - i), nb, stride=n)
                    sst_re[d, rows, :] = sr
                    sst_im[d, rows, :] = si
                    nxt.append((ar * sr - ai * si + sin_re[d, rows, :],
                                ar * si + ai * sr + sin_im[d, rows, :]))
                return tuple(nxt)
            return body

        z = jnp.zeros((nb, P), F32)
        carry = lax.fori_loop(0, nc_ctx, make_body(0, nc_ctx), ((z, z), (z, z)))
        lax.fori_loop(0, nc_lat, make_body(rc, nc_lat), carry)
        y = None
        for d in range(2):
            kt = kt_ref[d, gi]
            rows_tt = []
            for s in range(Lc):
                lag = [(t - s) if d == 0 else (s - t) for t in range(Lc)]
                rows_tt.append(jnp.concatenate(
                    [kt[:, k * hg:(k + 1) * hg] if k >= 0 else zero_blk for k in lag], axis=1))
            tt = jnp.concatenate(rows_tt, axis=0)
            o = jnp.concatenate([ca_ref[d, t + 1 if d == 0 else Lc - t, gi] for t in range(Lc)], axis=0)
            sst = jnp.concatenate([sst_re[d], sst_im[d]], axis=1)
            yd = jnp.dot(u, tt.astype(BF16), preferred_element_type=F32)
            yd += _bdot(sst, o, ((1,), (1,)))
            y = yd if y is None else y + yd
        y_scr[gi] = y
    for t in range(Lc):
        piece = jnp.concatenate([y_scr[gi, :, t * hg:(t + 1) * hg] for gi in range(n_g)], axis=1)
        yc_ref[:, t, :] = piece[:rc]
        yl_ref[:, t, :] = piece[rc:]


def _s5_scan(p_ctx, p_lat, B, kt, ca, ab, alre, alim):
    G, P, W, Lc = kt.shape[1], S5_STATE, S5_ROW, S5_CHUNK
    gb = LANES // S5_GROUP
    rc, rl = p_ctx.shape[0] // Lc, p_lat.shape[0] // Lc
    xc = p_ctx.reshape(rc, Lc, p_ctx.shape[1])
    xl = p_lat.reshape(rl, Lc, p_lat.shape[1])
    par = lambda *s: pl.BlockSpec((2, gb) + s, lambda j: (0, j) + (0,) * len(s))
    lag = lambda n, *s: pl.BlockSpec((2, n, gb) + s, lambda j: (0, 0, j) + (0,) * len(s))
    x3 = lambda r: pl.BlockSpec((r, Lc, LANES), lambda j: (0, 0, j))
    R = rc + rl
    yc, yl = pl.pallas_call(
        functools.partial(_s5_scan_kernel, nb=B, nc_ctx=rc // B, nc_lat=rl // B),
        grid=(G // gb,),
        in_specs=[x3(rc), x3(rl), par(S5_GROUP, Lc * S5_GROUP), lag(Lc + 1, S5_GROUP, 2 * P),
                  lag(Lc, S5_GROUP, 2 * P), par(1, P), par(1, P)],
        out_specs=[x3(rc), x3(rl)],
        out_shape=[jax.ShapeDtypeStruct((rc, Lc, G * S5_GROUP), F32),
                   jax.ShapeDtypeStruct((rl, Lc, G * S5_GROUP), F32)],
        scratch_shapes=[pltpu.VMEM((R, W), F32), pltpu.VMEM((gb, R, W), F32)]
                       + [pltpu.VMEM((2, R, P), F32)] * 4,
        compiler_params=_cparams(("parallel",)),
        name="s5_scan",
    )(xc, xl, kt, ca, ab, alre, alim)
    return yc.reshape(rc * Lc, -1), yl.reshape(rl * Lc, -1)


def _s5_out_kernel(y_ref, u_ref, d_ref, w_ref, b_ref, o_ref):
    x = y_ref[...] + d_ref[...] * u_ref[...]
    y = 0.5 * x * (1.0 + jnp.tanh(math.sqrt(2.0 / math.pi) * (x + 0.044715 * (x * x * x))))
    gate = _bdot(y, w_ref[...]) + b_ref[...]
    o_ref[...] = (y * _sigmoid(gate)).astype(o_ref.dtype)


def _s5_out(y, p, d, w, b, tm):
    M, W = y.shape
    row = lambda i: (i, 0)
    full = lambda i: (0, 0)
    return pl.pallas_call(
        _s5_out_kernel,
        grid=(M // tm,),
        in_specs=[pl.BlockSpec((tm, W), row), pl.BlockSpec((tm, W), row),
                  pl.BlockSpec((1, W), full), pl.BlockSpec((W, W), full),
                  pl.BlockSpec((1, W), full)],
        out_specs=pl.BlockSpec((tm, W), row),
        out_shape=jax.ShapeDtypeStruct((M, W), BF16),
        compiler_params=_cparams(("parallel",)),
        name="s5_out",
    )(y, p, d, w, b)


def _s5_mixer(p_ctx, p_lat, B, lam_re, lam_im, log_dt, b_re, b_im, c_re, c_im, d, glu_w, glu_b):
    G, Hg, P, Lc = lam_re.shape[1], S5_GROUP, S5_STATE, S5_CHUNK
    W = G * Hg
    kt, ca, ab, alre, alim = _s5_params(
        lam_re, lam_im, log_dt, b_re.swapaxes(-1, -2), b_im.swapaxes(-1, -2), c_re, c_im)
    y_c, y_l = _s5_scan(p_ctx, p_lat, B, kt, ca, ab, alre, alim)
    dd = d.reshape(1, W)
    gw = glu_w.astype(BF16)
    gb = glu_b.reshape(1, W)
    return (_s5_out(y_c, p_ctx, dd, gw, gb, min(1024, y_c.shape[0])),
            _s5_out(y_l, p_lat, dd, gw, gb, min(1024, y_l.shape[0])))


LANES = 128
GATE_SLOT = 8


def _dot_exact_lhs(b01, a):
    a0, a1, a2 = _split3(a)
    b = b01.astype(BF16)
    d = functools.partial(jnp.dot, preferred_element_type=F32)
    return d(b, a0) + (d(b, a1) + d(b, a2))


def _rt_window(heads):
    start = (6 * heads) // SUBLANES * SUBLANES
    assert 8 * heads <= start + 2 * GATE_SLOT
    return start, 6 * heads - start


def _gates_kernel(xc_ref, xl_ref, pa_ref, pb_ref, gg_ref, gm_ref, cumt_ref, rt_ref, btm_ref,
                  *, tc, tl, heads):
    blk, H = SUPER, heads
    n2 = 2 * H
    per = blk // CHUNK
    cols_per_blk = blk // (tl // GRID_W)
    ri = lax.broadcasted_iota(jnp.int32, (blk, blk), 0)
    ci = lax.broadcasted_iota(jnp.int32, (blk, blk), 1)
    same = (ri // CHUNK) == (ci // CHUNK)
    tri_f = (same & (ri >= ci)).astype(BF16)
    tri_b = (same & (ri <= ci)).astype(BF16)
    fwd_lane = (lax.broadcasted_iota(jnp.int32, (blk, LANES), 1) % n2) < H
    src = lax.broadcasted_iota(jnp.int32, (LANES, LANES), 0)
    dst = lax.broadcasted_iota(jnp.int32, (LANES, LANES), 1)

    def perm(base, q):
        rel = src - base
        tgt = (rel % H) * (2 * GATE_SLOT) + (rel // H) * GATE_SLOT + q
        return ((rel >= 0) & (rel < n2) & (dst == tgt)).astype(BF16)

    i_to_f = ((src >= 2 * n2) & (src < 3 * n2) & (dst == src + n2)).astype(BF16)
    perms_g = [perm(n2, 0)] + [perm(0, q) for q in (1, 2, 3, 4)]
    perms_m = [perm(3 * n2, q) for q in (0, 1, 2)]
    rt0, _ = _rt_window(H)
    pa, pb = pa_ref[...], pb_ref[...]
    zpad = jnp.zeros((blk, LANES - 2 * GATE_SLOT), F32)

    def cumsums(x):
        f = _dot_exact_lhs(tri_f, x)
        b = _dot_exact_lhs(tri_b, x)
        return jnp.where(fwd_lane, f, b), f + b - x

    def place(xs, perms, o_ref):
        out = None
        for x, p in zip(xs, perms):
            t = _dot_exact_rhs(x, p)
            out = t if out is None else out + t
        for h in range(H):
            piece = out[:, h * 2 * GATE_SLOT:(h + 1) * 2 * GATE_SLOT]
            o_ref[:, h * LANES:(h + 1) * LANES] = jnp.concatenate([piece, zpad], axis=1)

    def emit(x_seq, x_scan):
        g = -jnp.exp(pa) * _softplus(x_seq + pb)
        cum, tot = cumsums(g)
        place([_sigmoid(x_seq), cum, jnp.exp(cum), jnp.exp(tot - cum), jnp.exp(tot)], perms_g, gg_ref)
        cumt_ref[...] = cum.T[0:2 * GATE_SLOT, :]
        xb = x_scan + pb
        ig = _dot_exact_rhs(xb, i_to_f)
        b, bt = cumsums(-_softplus(-xb))
        log_w = bt - b + ig
        mch = jnp.max(log_w.reshape(per, CHUNK, LANES), axis=1, keepdims=True)
        ew = jnp.exp(log_w.reshape(per, CHUNK, LANES) - mch).reshape(blk, LANES)
        r = ig - b
        place([b, ew, r], perms_m, gm_ref)
        rt_ref[...] = r.T[rt0:rt0 + 2 * GATE_SLOT, :]
        for j in range(per):
            btm_ref[j, 0:1, :] = bt[j * CHUNK:j * CHUNK + 1, :]
            btm_ref[j, 1:2, :] = mch[j]

    sc = pl.program_id(1)
    n_ctx = tc // blk

    @pl.when(sc < n_ctx)
    def _():
        x = xc_ref[pl.ds(pl.multiple_of(sc * blk, blk), blk), :]
        emit(x, x)

    @pl.when(sc >= n_ctx)
    def _():
        s = sc - n_ctx
        x_seq = xl_ref[pl.ds(pl.multiple_of(s * blk, blk), blk), :]
        x_scan = jnp.concatenate(
            [xl_ref[pl.ds(s * cols_per_blk + j, tl // GRID_W, stride=GRID_W), :]
             for j in range(cols_per_blk)], axis=0)
        emit(x_seq, x_scan)


def _gates(p_ctx, p_lat, B, gate0, pa, pb, heads):
    tc, tl = p_ctx.shape[0] // B, p_lat.shape[0] // B
    tt = tc + tl
    nsc = tt // SUPER
    per = SUPER // CHUNK
    gblk = gate0 // LANES
    tok = pl.BlockSpec((None, SUPER, heads * LANES), lambda b, s: (b, s, 0))
    rowf = pl.BlockSpec((None, None, 2 * GATE_SLOT, SUPER), lambda b, s: (b, s, 0, 0))
    return pl.pallas_call(
        functools.partial(_gates_kernel, tc=tc, tl=tl, heads=heads),
        grid=(B, nsc),
        in_specs=[pl.BlockSpec((tc, LANES), lambda b, s: (b, gblk)),
                  pl.BlockSpec((tl, LANES), lambda b, s: (b, gblk)),
                  pl.BlockSpec((1, LANES), lambda b, s: (0, 0)),
                  pl.BlockSpec((1, LANES), lambda b, s: (0, 0))],
        out_specs=[tok, tok, rowf, rowf,
                   pl.BlockSpec((None, per, 2, LANES), lambda b, s: (b, s, 0, 0))],
        out_shape=[jax.ShapeDtypeStruct((B, tt, heads * LANES), F32),
                   jax.ShapeDtypeStruct((B, tt, heads * LANES), F32),
                   jax.ShapeDtypeStruct((B, nsc, 2 * GATE_SLOT, SUPER), F32),
                   jax.ShapeDtypeStruct((B, nsc, 2 * GATE_SLOT, SUPER), F32),
                   jax.ShapeDtypeStruct((B, tt // CHUNK, 2, LANES), F32)],
        compiler_params=_cparams(("parallel", "parallel")),
        name="gates",
    )(p_ctx, p_lat, pa, pb)


def _gdn_kernel(qc_ref, kc_ref, vc_ref, zc_ref, ql_ref, kl_ref, vl_ref, zl_ref,
                cwq_ref, cwk_ref, cwv_ref, gg_ref, cumt_ref, gn_ref,
                oc_ref, ol_ref, pad_scr, qn, kn, vn, a_scr, b_scr, qp_scr, op_scr, o_scr,
                a2_scr, b2_scr, *, tc, tl, heads):
    dk = HEAD_DIM
    blk = SUPER
    head = pl.program_id(1)

    def conv(src_ref, w_ref, dst, off, t, norm):
        pad_scr[0:8, :] = jnp.zeros((8, dk), F32)
        pad_scr[8:8 + t, :] = src_ref[...]
        pad_scr[8 + t:16 + t, :] = jnp.zeros((8, dk), F32)
        w0, w1, w2 = w_ref[0:1, :], w_ref[1:2, :], w_ref[2:3, :]
        for r0 in range(0, t, blk):
            y = (w0 * pad_scr[r0 + 7:r0 + 7 + blk, :] + w1 * pad_scr[r0 + 8:r0 + 8 + blk, :]
                 + w2 * pad_scr[r0 + 9:r0 + 9 + blk, :])
            y = _silu(y)
            if norm is not None:
                y = y * lax.rsqrt(jnp.sum(y * y, axis=-1, keepdims=True) + EPS) * norm
            dst[off + r0:off + r0 + blk, :] = y

    for src_c, src_l, w_ref, dst, norm in ((qc_ref, ql_ref, cwq_ref, qn, dk ** -0.5),
                                           (kc_ref, kl_ref, cwk_ref, kn, 1.0),
                                           (vc_ref, vl_ref, cwv_ref, vn, None)):
        conv(src_c, w_ref, dst, 0, tc, norm)
        conv(src_l, w_ref, dst, tc, tl, norm)

    n_sc = (tc + tl) // blk
    ncc, ncl = tc // CHUNK, tl // CHUNK
    per = blk // CHUNK
    nt = ((1,), (1,))

    ri = lax.broadcasted_iota(jnp.int32, (blk, blk), 0)
    ci = lax.broadcasted_iota(jnp.int32, (blk, blk), 1)
    same = (ri // CHUNK) == (ci // CHUNK)
    eye_b = (ri == ci).astype(F32)
    tri_m = (same & (ri >= ci), same & (ri <= ci))
    strict_m = (same & (ri > ci), same & (ri < ci))
    lr = lax.broadcasted_iota(jnp.int32, (per * dk, blk), 0)
    lc = lax.broadcasted_iota(jnp.int32, (per * dk, blk), 1)
    chunk_sel = (lr // dk) == (lc // CHUNK)

    def prepare(scs):
        r0s = [pl.multiple_of(sc * blk, blk) for sc in scs]
        qs = [qn[pl.ds(r0, blk), :] for r0 in r0s]
        ks = [kn[pl.ds(r0, blk), :] for r0 in r0s]
        vs = [vn[pl.ds(r0, blk), :] for r0 in r0s]
        qkts = [_bdot(q, k, nt) for q, k in zip(qs, ks)]
        gts = [gg_ref[pl.ds(r0, blk), :] for r0 in r0s]
        items = [(i, d) for i in range(len(scs)) for d in (0, 1)]
        cgs = [gts[i][:, d * GATE_SLOT:(d + 1) * GATE_SLOT] for i, d in items]
        crows = [cumt_ref[scs[i], pl.ds(d * heads + head, 1), :] for i, d in items]
        decs = [jnp.where(tri_m[d], jnp.exp(jnp.where(tri_m[d], cg[:, 1:2] - crow, 0.0)), 0.0)
                for (i, d), cg, crow in zip(items, cgs, crows)]
        kbs = [ks[i] * cg[:, 0:1] for (i, d), cg in zip(items, cgs)]
        ms = [jnp.where(strict_m[d], _bdot(kb, ks[i], nt) * dec, 0.0)
              for (i, d), kb, dec in zip(items, kbs, decs)]
        ps = [eye_b - jnp.where((ri // 2) == (ci // 2), m, 0.0) for m in ms]
        n = 2
        while n < CHUNK:
            join = ((ri // (2 * n)) == (ci // (2 * n))) & ((ri // n) != (ci // n))
            ps = [p - _bdot(p, _bdot(jnp.where(join, m, 0.0), p)) for p, m in zip(ps, ms)]
            n *= 2
        uws = [_bdot(p, jnp.concatenate([vs[i] * cg[:, 0:1], kb * cg[:, 2:3]], axis=1))
               for (i, d), p, cg, kb in zip(items, ps, cgs, kbs)]
        ows = [_bdot(jnp.where(tri_m[d], qkts[i] * dec, 0.0), uw)
               for (i, d), dec, uw in zip(items, decs, uws)]
        lhss = [jnp.where(chunk_sel, jnp.concatenate([(ks[i] * cg[:, 3:4]).T] * per, axis=0), 0.0)
                for (i, d), cg in zip(items, cgs)]
        abs_ = [_bdot(lhs, uw) for lhs, uw in zip(lhss, uws)]
        for (i, d), cg, ow, ab in zip(items, cgs, ows, abs_):
            ra = pl.multiple_of(scs[i] * (per * dk), per * dk)
            qp_scr[d, pl.ds(r0s[i], blk), :] = (qs[i] * cg[:, 2:3] - ow[:, dk:]).astype(BF16)
            op_scr[d, pl.ds(r0s[i], blk), :] = ow[:, :dk]
            a_scr[d, pl.ds(ra, per * dk), :] = ab[:, dk:].astype(BF16)
            b_scr[d, pl.ds(ra, per * dk), :] = ab[:, :dk]
            for m in range(per // 2):
                j0, j1 = (2 * m, 2 * m + 1) if d == 0 else (2 * m + 1, 2 * m)
                e0 = cg[j0 * CHUNK:j0 * CHUNK + 1, 4:5]
                e1 = cg[j1 * CHUNK:j1 * CHUNK + 1, 4:5]
                ab0 = ab[j0 * dk:(j0 + 1) * dk, :]
                ab1 = ab[j1 * dk:(j1 + 1) * dk, :]
                x = _bdot(ab1[:, dk:], ab0)
                rp = pl.multiple_of((scs[i] * (per // 2) + m) * dk, dk)
                a2_scr[d, pl.ds(rp, dk), :] = (e1 * ab0[:, dk:] + e0 * ab1[:, dk:] - x[:, dk:]).astype(BF16)
                b2_scr[d, pl.ds(rp, dk), :] = e1 * ab0[:, :dk] + ab1[:, :dk] - x[:, :dk]

    def prepare_one(sc, _):
        prepare((sc,))
        return 0

    def prepare_two(i, _):
        prepare((tc // blk + 2 * i, tc // blk + 2 * i + 1))
        return 0

    n_lat = tl // blk
    lax.fori_loop(0, tc // blk, prepare_one, 0)
    lax.fori_loop(0, n_lat // 2, prepare_two, 0)
    if n_lat % 2:
        prepare_one(n_sc - 1, 0)

    def make_step(n, off_p):
        def step(i, carry):
            nxt = []
            for d in range(2):
                s = carry[d]
                p = off_p + (i if d == 0 else n - 1 - i)
                sb = s.astype(BF16)

                def emit(c, state_bf16):
                    r0 = pl.multiple_of(c * CHUNK, CHUNK)
                    o = jnp.dot(qp_scr[d, pl.ds(r0, CHUNK), :], state_bf16, preferred_element_type=F32)
                    o_scr[d, pl.ds(r0, CHUNK), :] = o + op_scr[d, pl.ds(r0, CHUNK), :]
                    return gg_ref[pl.ds(r0, 1), :][:, d * GATE_SLOT + 4:d * GATE_SLOT + 5]

                c0, c1 = 2 * p + d, 2 * p + 1 - d
                e0 = emit(c0, sb)
                ra = pl.multiple_of(c0 * dk, dk)
                s1 = (s * e0 + b_scr[d, pl.ds(ra, dk), :]
                      - jnp.dot(a_scr[d, pl.ds(ra, dk), :], sb, preferred_element_type=F32))
                e1 = emit(c1, s1.astype(BF16))
                rp = pl.multiple_of(p * dk, dk)
                nxt.append(s * (e0 * e1) + b2_scr[d, pl.ds(rp, dk), :]
                           - jnp.dot(a2_scr[d, pl.ds(rp, dk), :], sb, preferred_element_type=F32))
            return tuple(nxt)
        return step

    s0 = jnp.zeros((dk, dk), F32)
    carry = lax.fori_loop(0, ncc // 2, make_step(ncc // 2, 0), (s0, s0), unroll=2)
    lax.fori_loop(0, ncl // 2, make_step(ncl // 2, ncc // 2), carry, unroll=2)

    gn = gn_ref[...]
    for z_ref, o_ref, off, t in ((zc_ref, oc_ref, 0, tc), (zl_ref, ol_ref, tc, tl)):
        for r0 in range(0, t, blk):
            o = o_scr[0, off + r0:off + r0 + blk, :] + o_scr[1, off + r0:off + r0 + blk, :]
            o_ref[r0:r0 + blk, :] = (_rms(o, gn) * _silu(z_ref[r0:r0 + blk, :])).astype(o_ref.dtype)


def _gdn(p_ctx, p_lat, B, conv_w, gg, cumt, gnorm, H):
    tc, tl = p_ctx.shape[0] // B, p_lat.shape[0] // B
    tt = tc + tl
    q0, k0, v0, z0 = 4, 4 + H, 4 + 2 * H, 4 + 3 * H

    def col(t, c0):
        return pl.BlockSpec((t, HEAD_DIM), lambda b, h: (b, c0 + h))

    return pl.pallas_call(
        functools.partial(_gdn_kernel, tc=tc, tl=tl, heads=H),
        grid=(B, H),
        in_specs=[col(tc, q0), col(tc, k0), col(tc, v0), col(tc, z0),
                  col(tl, q0), col(tl, k0), col(tl, v0), col(tl, z0),
                  pl.BlockSpec((3, HEAD_DIM), lambda b, h: (0, h)),
                  pl.BlockSpec((3, HEAD_DIM), lambda b, h: (0, H + h)),
                  pl.BlockSpec((3, HEAD_DIM), lambda b, h: (0, 2 * H + h)),
                  pl.BlockSpec((None, tt, LANES), lambda b, h: (b, 0, h)),
                  pl.BlockSpec((None,) + cumt.shape[1:], lambda b, h: (b, 0, 0, 0)),
                  pl.BlockSpec((1, HEAD_DIM), lambda b, h: (0, 0))],
        out_specs=[pl.BlockSpec((tc, HEAD_DIM), lambda b, h: (b, h)),
                   pl.BlockSpec((tl, HEAD_DIM), lambda b, h: (b, h))],
        out_shape=[jax.ShapeDtypeStruct((B * tc, H * HEAD_DIM), BF16),
                   jax.ShapeDtypeStruct((B * tl, H * HEAD_DIM), BF16)],
        scratch_shapes=[pltpu.VMEM((max(tc, tl) + 16, HEAD_DIM), F32)]
                       + [pltpu.VMEM((tt, HEAD_DIM), F32)] * 3
                       + [pltpu.VMEM((2, tt // CHUNK * HEAD_DIM, HEAD_DIM), BF16),
                          pltpu.VMEM((2, tt // CHUNK * HEAD_DIM, HEAD_DIM), F32),
                          pltpu.VMEM((2, tt, HEAD_DIM), BF16),
                          pltpu.VMEM((2, tt, HEAD_DIM), F32),
                          pltpu.VMEM((2, tt, HEAD_DIM), F32),
                          pltpu.VMEM((2, tt // (2 * CHUNK) * HEAD_DIM, HEAD_DIM), BF16),
                          pltpu.VMEM((2, tt // (2 * CHUNK) * HEAD_DIM, HEAD_DIM), F32)],
        compiler_params=_cparams(("parallel", "parallel")),
        name="gdn",
    )(p_ctx, p_ctx, p_ctx, p_ctx, p_lat, p_lat, p_lat, p_lat, conv_w, conv_w, conv_w,
      gg, cumt, gnorm)


def _mlstm_kernel(qkc_ref, vc_ref, oc_ref, qkl_ref, vl_ref, ol_ref, gm_ref, rt_ref,
                  btm_ref, gn_ref, hc_ref, hl_ref, na_scr, kva_scr, qs_scr, ms_scr, h_scr, sc_scr,
                  *, tc, tl, heads):
    blk, dk, dv = SUPER, ML_DK, HEAD_DIM
    per = blk // CHUNK
    ncc, ncl = tc // CHUNK, tl // CHUNK
    qscale = dk ** -0.5
    head = pl.program_id(1)
    rt_off = _rt_window(heads)[1]
    grid_rows = tl // GRID_W
    cols_per_blk = blk // grid_rows

    def seq_rows(ref, s):
        return ref[pl.ds(pl.multiple_of(s * blk, blk), blk), :]

    def grid_col_rows(ref, s):
        return jnp.concatenate([ref[pl.ds(s * cols_per_blk + j, grid_rows, stride=GRID_W), :]
                                for j in range(cols_per_blk)], axis=0)

    lane_n = lax.broadcasted_iota(jnp.int32, (ncc + ncl, LANES), 1)
    for d in range(2):
        pick = lane_n == 6 * heads + d * heads + head
        for j in range(2):
            col = jnp.sum(jnp.where(pick, btm_ref[:, j, :], 0.0), axis=1, keepdims=True)
            sc_scr[d, j] = jnp.broadcast_to(col, (ncc + ncl, 2 * dv))

    ri = lax.broadcasted_iota(jnp.int32, (blk, blk), 0)
    ci = lax.broadcasted_iota(jnp.int32, (blk, blk), 1)
    same = (ri // CHUNK) == (ci // CHUNK)
    tri_m = (same & (ri >= ci), same & (ri <= ci))
    lane2 = lax.broadcasted_iota(jnp.int32, (blk, 2 * dv), 1)
    one_col = (lax.broadcasted_iota(jnp.int32, (blk, dv), 1) == 0).astype(F32)

    for d in range(2):
        def make_scalar_step(n, off_c):
            def step(i, m_st):
                c = off_c + (i if d == 0 else n - 1 - i)
                bt = sc_scr[d, 0, pl.ds(c, 1), :]
                mch = sc_scr[d, 1, pl.ds(c, 1), :]
                m_new = jnp.maximum(bt + m_st, mch)
                ms_scr[d, c, 0:1, :] = m_st
                ms_scr[d, c, 1:2, :] = jnp.exp(bt + m_st - m_new)
                ms_scr[d, c, 2:3, :] = jnp.exp(mch - m_new)
                return m_new
            return step
        m_c = lax.fori_loop(0, ncc, make_scalar_step(ncc, 0), jnp.zeros((1, 2 * dv), F32))
        lax.fori_loop(0, ncl, make_scalar_step(ncl, ncc), m_c)

    rblk = lax.broadcasted_iota(jnp.int32, (blk, 1), 0) // CHUNK

    def per_chunk_col(vals):
        col = vals[per - 1]
        for j in range(per - 2, -1, -1):
            col = jnp.where(rblk == j, vals[j], col)
        return col

    def prepare(qk_ref, v_ref, off_sc, rows, ss):
        scs = [off_sc + s for s in ss]
        r0s = [pl.multiple_of(sc * blk, blk) for sc in scs]
        qks = [rows(qk_ref, s) for s in ss]
        qs = [qk[:, :dk] * qscale for qk in qks]
        ks = [qk[:, dk:] for qk in qks]
        vas = [jnp.concatenate([rows(v_ref, s), one_col], axis=1) for s in ss]
        qkts = [_bdot(q, k, ((1,), (1,))) for q, k in zip(qs, ks)]
        cms = [gm_ref[pl.ds(r0, blk), :] for r0 in r0s]
        items = [(i, d) for i in range(len(ss)) for d in (0, 1)]
        b_cols = [cms[i][:, d * GATE_SLOT:d * GATE_SLOT + 1] for i, d in items]
        ews = [cms[i][:, d * GATE_SLOT + 1:d * GATE_SLOT + 2] for i, d in items]
        r_rows = [rt_ref[scs[i], pl.ds(rt_off + d * heads + head, 1), :] for i, d in items]
        mss = [[ms_scr[d, scs[i] * per + j] for j in range(per)] for i, d in items]
        m_sts = [per_chunk_col([m[0:1, 0:1] for m in ms]) for ms in mss]
        a_news = [per_chunk_col([m[2:3, 0:1] for m in ms]) for ms in mss]
        log_ds = [jnp.where(tri_m[d], b + r, -jnp.inf) for (i, d), b, r in zip(items, b_cols, r_rows)]
        m_ts = [jnp.maximum(b + m, jnp.max(ld, axis=-1, keepdims=True))
                for b, m, ld in zip(b_cols, m_sts, log_ds)]
        nas = [_bdot(qkts[i] * jnp.exp(ld - mt), vas[i])
               for (i, d), ld, mt in zip(items, log_ds, m_ts)]
        lhss = [jnp.where(same, jnp.concatenate([(ks[i] * (ew * an)).T] * per, axis=0), 0.0)
                for (i, d), ew, an in zip(items, ews, a_news)]
        kvas = [_bdot(lhs, vas[i]) for (i, d), lhs in zip(items, lhss)]
        for n, (i, d) in enumerate(items):
            na_scr[d, pl.ds(r0s[i], blk), :] = jnp.where(lane2 == dv + 1, jnp.exp(-m_ts[n]), nas[n])
            qs_scr[d, pl.ds(r0s[i], blk), :] = (
                qs[i] * jnp.exp(b_cols[n] + m_sts[n] - m_ts[n])).astype(BF16)
            kva_scr[d, pl.ds(r0s[i], blk), :] = kvas[n]

    def prepare_ctx(s, _):
        prepare(qkc_ref, vc_ref, 0, seq_rows, (s,))
        return 0

    def prepare_lat_pair(i, _):
        prepare(qkl_ref, vl_ref, tc // blk, grid_col_rows, (2 * i, 2 * i + 1))
        return 0

    n_lat = tl // blk
    lax.fori_loop(0, tc // blk, prepare_ctx, 0)
    lax.fori_loop(0, n_lat // 2, prepare_lat_pair, 0)
    if n_lat % 2:
        prepare(qkl_ref, vl_ref, tc // blk, grid_col_rows, (n_lat - 1,))

    def make_step(n, off_c):
        def step(i, carry):
            nxt = []
            for d in range(2):
                c_st = carry[d]
                c = off_c + (i if d == 0 else n - 1 - i)
                r0 = pl.multiple_of(c * CHUNK, CHUNK)
                na = na_scr[d, pl.ds(r0, CHUNK), :]
                tot = na + jnp.dot(qs_scr[d, pl.ds(r0, CHUNK), :], c_st.astype(BF16),
                                   preferred_element_type=F32)
                den = jnp.maximum(jnp.abs(tot[:, dv:dv + 1]), na[:, dv + 1:dv + 2])
                h_scr[d, pl.ds(r0, CHUNK), :] = tot[:, :dv] / den
                nxt.append(ms_scr[d, c, 1:2, :] * c_st + kva_scr[d, pl.ds(r0, CHUNK), :])
            return tuple(nxt)
        return step

    st0 = jnp.zeros((dk, 2 * dv), F32)
    carry = lax.fori_loop(0, ncc, make_step(ncc, 0), (st0, st0), unroll=4)
    lax.fori_loop(0, ncl, make_step(ncl, ncc), carry, unroll=4)

    gn = gn_ref[...]

    def gated(r0, og):
        h = h_scr[0, r0:r0 + blk, :] + h_scr[1, r0:r0 + blk, :]
        return (_rms(h, gn) * _sigmoid(og)).astype(hc_ref.dtype)

    for s in range(tc // blk):
        hc_ref[s * blk:(s + 1) * blk, :] = gated(s * blk, oc_ref[s * blk:(s + 1) * blk, :])
    for s in range(tl // blk):
        res = gated(tc + s * blk, grid_col_rows(ol_ref, s))
        for j in range(cols_per_blk):
            hl_ref[pl.ds(s * cols_per_blk + j, grid_rows, stride=GRID_W), :] = (
                res[j * grid_rows:(j + 1) * grid_rows])


def _mlstm(p_ctx, p_lat, B, gm, rt, btm, gnorm, qk0, H):
    tc, tl = p_ctx.shape[0] // B, p_lat.shape[0] // B
    tt = tc + tl

    def col(t, c0):
        return pl.BlockSpec((t, HEAD_DIM), lambda b, h: (b, c0 + h))

    return pl.pallas_call(
        functools.partial(_mlstm_kernel, tc=tc, tl=tl, heads=H),
        grid=(B, H),
        in_specs=[col(tc, qk0), col(tc, qk0 + H), col(tc, qk0 + 2 * H),
                  col(tl, qk0), col(tl, qk0 + H), col(tl, qk0 + 2 * H),
                  pl.BlockSpec((None, tt, LANES), lambda b, h: (b, 0, h)),
                  pl.BlockSpec((None,) + rt.shape[1:], lambda b, h: (b, 0, 0, 0)),
                  pl.BlockSpec((None,) + btm.shape[1:], lambda b, h: (b, 0, 0, 0)),
                  pl.BlockSpec((1, HEAD_DIM), lambda b, h: (0, 0))],
        out_specs=[pl.BlockSpec((tc, HEAD_DIM), lambda b, h: (b, h)),
                   pl.BlockSpec((tl, HEAD_DIM), lambda b, h: (b, h))],
        out_shape=[jax.ShapeDtypeStruct((B * tc, H * HEAD_DIM), F32),
                   jax.ShapeDtypeStruct((B * tl, H * HEAD_DIM), F32)],
        scratch_shapes=[pltpu.VMEM((2, tt, 2 * HEAD_DIM), F32)] * 2
                       + [pltpu.VMEM((2, tt, ML_DK), BF16),
                          pltpu.VMEM((2, tt // CHUNK, 3, 2 * HEAD_DIM), F32),
                          pltpu.VMEM((2, tt, HEAD_DIM), F32),
                          pltpu.VMEM((2, 2, tt // CHUNK, 2 * HEAD_DIM), F32)],
        compiler_params=_cparams(("parallel", "parallel")),
        name="mlstm",
    )(p_ctx, p_ctx, p_ctx, p_lat, p_lat, p_lat, gm, rt, btm, gnorm)


def _in_sizes(D):
    s5 = D // 4
    gw = 3 * D // 8
    gh = gw // HEAD_DIM
    mw = D - s5 - gw
    mh = mw // HEAD_DIM
    return s5, gw, gh, mw, mh


def _permute_w_in(w, D):
    s5, gw, gh, mw, mh = _in_sizes(D)
    o_a = s5 + 4 * gw
    o_mq = o_a + 4 * gh
    o_mk = o_mq + mh * ML_DK
    o_mv = o_mk + mh * ML_DK
    o_mi = o_mv + 2 * mw
    end = o_mi + 4 * mh
    pieces = [w[:, :o_a]]
    for h in range(mh):
        pieces += [w[:, o_mq + h * ML_DK:o_mq + (h + 1) * ML_DK],
                   w[:, o_mk + h * ML_DK:o_mk + (h + 1) * ML_DK]]
    pieces += [w[:, o_mv:o_mi], w[:, o_a:o_mq], w[:, o_mi:end]]
    used = o_a + 2 * mh * ML_DK + 2 * mw + 4 * gh + 4 * mh
    total = -(-used // 512) * 512
    pieces.append(jnp.zeros((w.shape[0], total - used), w.dtype))
    gate0 = o_a + 2 * mh * ML_DK + 2 * mw
    return jnp.concatenate(pieces, axis=1).astype(BF16), gate0


def _token_mixer(p_ctx, p_lat, B, gate0, prm):
    (s5_lam_re, s5_lam_im, s5_log_dt, s5_b_re, s5_b_im, s5_c_re, s5_c_im, s5_d, s5_glu_w,
     s5_glu_b, gdn_conv_w, gdn_a_log, gdn_dt_bias, gdn_norm, ml_i_bias, ml_f_bias, ml_norm) = prm
    tc, tl = p_ctx.shape[0] // B, p_lat.shape[0] // B
    rows = tl // GRID_W
    H = gdn_a_log.shape[-1]
    nch = (tc + tl) // CHUNK

    s5_c, s5_l = _s5_mixer(p_ctx, p_lat, B, s5_lam_re, s5_lam_im, s5_log_dt, s5_b_re, s5_b_im,
                           s5_c_re, s5_c_im, s5_d, s5_glu_w, s5_glu_b)

    n2 = 2 * H
    zero = jnp.zeros((n2,), F32)
    tail = jnp.zeros((LANES - 4 * n2,), F32)
    pa = jnp.concatenate([gdn_a_log.reshape(-1), zero, zero, zero, tail]).reshape(1, LANES)
    pb = jnp.concatenate([gdn_dt_bias.reshape(-1), zero, ml_i_bias.reshape(-1),
                          ml_f_bias.reshape(-1), tail]).reshape(1, LANES)
    gg, gm, cumt, rt, btm = _gates(p_ctx, p_lat, B, gate0, pa, pb, H)

    gd_c, gd_l = _gdn(p_ctx, p_lat, B, gdn_conv_w, gg, cumt, gdn_norm.reshape(1, HEAD_DIM), H)
    qk0 = gate0 // HEAD_DIM - 3 * H
    ml_c, ml_l = _mlstm(p_ctx, p_lat, B, gm, rt, btm, ml_norm.reshape(1, HEAD_DIM), qk0, H)
    return (s5_c, gd_c, ml_c), (s5_l, gd_l, ml_l)


def kernel(x, c, ctx, c_ctx, ada_w, ada_b, norm_mix_pre, norm_mix_post, norm_ffn_pre, norm_ffn_post, w_in, w_out, s5_lam_re, s5_lam_im, s5_log_dt, s5_b_re, s5_b_im, s5_c_re, s5_c_im, s5_d, s5_glu_w, s5_glu_b, gdn_conv_w, gdn_a_log, gdn_dt_bias, gdn_norm, mlstm_i_bias, mlstm_f_bias, mlstm_norm, ffn_w_gate, ffn_w_up, ffn_w_down):
    B, T, D = x.shape
    TC = ctx.shape[1]
    L = ada_w.shape[0]
    assert T % SUPER == 0 and TC % SUPER == 0 and T % GRID_W == 0 and B <= SUBLANES - 1
    s5w, gw, _, _, _ = _in_sizes(D)

    x_lat = x.reshape(B * T, D)
    x_ctx = ctx.reshape(B * TC, D)
    c8 = jnp.concatenate([c, c_ctx[None], jnp.zeros((SUBLANES - B - 1, D), F32)], 0)
    mods = _ada(c8, ada_w, ada_b).reshape(L * SUBLANES, 1, 6 * D)

    tm_lat = 1024
    tm_ctx = min(1024, B * TC)
    tiles_per_batch = T // tm_lat

    for l in range(L):
        lat_row = lambda i, l=l: l * SUBLANES + i // tiles_per_batch
        ctx_row = lambda i, l=l: l * SUBLANES + B
        g_mix_pre = norm_mix_pre[l].reshape(1, D)
        g_mix_post = norm_mix_post[l].reshape(1, D)
        g_ffn_pre = norm_ffn_pre[l].reshape(1, D)
        g_ffn_post = norm_ffn_post[l].reshape(1, D)

        w_in_p, gate0 = _permute_w_in(w_in[l], D)
        p_lat = _inproj(x_lat, mods, lat_row, g_mix_pre, w_in_p, tm_lat, 1024)
        p_ctx = _inproj(x_ctx, mods, ctx_row, g_mix_pre, w_in_p, tm_ctx, 1024)

        prm = (s5_lam_re[l], s5_lam_im[l], s5_log_dt[l], s5_b_re[l], s5_b_im[l], s5_c_re[l],
               s5_c_im[l], s5_d[l], s5_glu_w[l], s5_glu_b[l], gdn_conv_w[l], gdn_a_log[l],
               gdn_dt_bias[l], gdn_norm[l], mlstm_i_bias[l], mlstm_f_bias[l], mlstm_norm[l])
        mix_ctx, mix_lat = _token_mixer(p_ctx, p_lat, B, gate0, prm)

        wo = w_out[l].astype(BF16)
        wa, wb, wc = wo[:s5w], wo[s5w:s5w + gw], wo[s5w + gw:]
        wg = ffn_w_gate[l].astype(BF16)
        wu = ffn_w_up[l].astype(BF16)
        wd = ffn_w_down[l].astype(BF16)

        lat_row_f = lambda i, l=l: l * SUBLANES + i // (T // 512)
        xs = _outproj(x_lat, mix_lat[0], mix_lat[1], mix_lat[2], mods, lat_row_f, g_mix_post,
                      wa, wb, wc, 512)
        x_lat = _ffn(xs, mods, lat_row_f, g_ffn_pre, g_ffn_post, wg, wu, wd, 512, 512)
        if l < L - 1:
            xs = _outproj(x_ctx, mix_ctx[0], mix_ctx[1], mix_ctx[2], mods, ctx_row, g_mix_post,
                          wa, wb, wc, 512)
            x_ctx = _ffn(xs, mods, ctx_row, g_ffn_pre, g_ffn_post, wg, wu, wd, 512, 512)
    return x_lat.reshape(B, T, D)
```

```python
import functools
import math

import jax
import jax.numpy as jnp
from jax import lax
from jax.experimental import pallas as pl
from jax.experimental.pallas import tpu as pltpu

F32 = jnp.float32
BF16 = jnp.bfloat16

EPS = 1e-6
GRID_W = 64
CHUNK = 64
SUPER = 4 * CHUNK
HEAD_DIM = 128
S5_GROUP = 16
S5_STATE = 64
S5_CHUNK = 16
S5_ROW = S5_CHUNK * S5_GROUP
SUBLANES = 8
ML_DK = 64

VMEM_LIMIT = 48 * 1024 * 1024


def _cparams(sem):
    return pltpu.CompilerParams(dimension_semantics=sem, vmem_limit_bytes=VMEM_LIMIT)


def _bdot(a, b, dims=((1,), (0,))):
    return lax.dot_general(a.astype(BF16), b.astype(BF16), (dims, ((), ())),
                           preferred_element_type=F32)


def _split3(a):
    a0 = a.astype(BF16)
    r1 = a - a0.astype(F32)
    a1 = r1.astype(BF16)
    a2 = (r1 - a1.astype(F32)).astype(BF16)
    return a0, a1, a2


def _dot3(a, b):
    a0, a1, _ = _split3(a)
    b0, b1, _ = _split3(b)
    d = functools.partial(jnp.dot, preferred_element_type=F32)
    return d(a0, b0) + (d(a0, b1) + d(a1, b0))


def _dot_exact_rhs(a, b01):
    a0, a1, a2 = _split3(a)
    b = b01.astype(BF16)
    d = functools.partial(jnp.dot, preferred_element_type=F32)
    return d(a0, b) + (d(a1, b) + d(a2, b))


def _sigmoid(x):
    return 1.0 / (1.0 + jnp.exp(-x))


def _silu(x):
    return x * _sigmoid(x)


def _softplus(x):
    return jnp.maximum(x, 0.0) + jnp.log(1.0 + jnp.exp(-jnp.abs(x)))


def _rms(x, g):
    return x * lax.rsqrt(jnp.mean(x * x, axis=-1, keepdims=True) + EPS) * g


CAST_BLOCK_BYTES = 8 * 1024 * 1024


def _cast_kernel(x_ref, o_ref):
    o_ref[...] = x_ref[...].astype(o_ref.dtype)


def _to_bf16(w, l):
    _, R, C = w.shape
    tr = R
    while tr * C * 4 > CAST_BLOCK_BYTES and tr % 2 == 0 and (tr // 2) % 16 == 0:
        tr //= 2
    return pl.pallas_call(
        _cast_kernel,
        grid=(R // tr,),
        in_specs=[pl.BlockSpec((None, tr, C), lambda i: (l, i, 0))],
        out_specs=pl.BlockSpec((tr, C), lambda i: (i, 0)),
        out_shape=jax.ShapeDtypeStruct((R, C), BF16),
        compiler_params=_cparams(("parallel",)),
        name="cast",
    )(w)


def _ada_kernel(c_ref, w_ref, b_ref, o_ref):
    c = c_ref[...]
    o_ref[...] = _bdot(_silu(c), w_ref[...]) + b_ref[...]


def _ada(c8, ada_w, ada_b):
    L, D, N = ada_w.shape
    tn = 1024
    return pl.pallas_call(
        _ada_kernel,
        grid=(L, N // tn),
        in_specs=[pl.BlockSpec((SUBLANES, D), lambda l, j: (0, 0)),
                  pl.BlockSpec((None, D, tn), lambda l, j: (l, 0, j)),
                  pl.BlockSpec((None, 1, tn), lambda l, j: (l, 0, j))],
        out_specs=pl.BlockSpec((None, SUBLANES, tn), lambda l, j: (l, 0, j)),
        out_shape=jax.ShapeDtypeStruct((L, SUBLANES, N), F32),
        compiler_params=_cparams(("parallel", "arbitrary")),
        name="ada",
    )(c8, ada_w, ada_b.reshape(L, 1, N))


def _inproj_kernel(x_ref, sh_ref, sc_ref, g_ref, w_ref, o_ref, h_scr):
    @pl.when(pl.program_id(1) == 0)
    def _():
        h = _rms(x_ref[...], g_ref[...]) * (1.0 + sc_ref[0]) + sh_ref[0]
        h_scr[...] = h.astype(BF16)

    o_ref[...] = jnp.dot(h_scr[...], w_ref[...], preferred_element_type=F32)


def _inproj(x, mods, mod_row, g_pre, w, tm, tn):
    M, D = x.shape
    N = w.shape[1]
    return pl.pallas_call(
        _inproj_kernel,
        grid=(M // tm, N // tn),
        in_specs=[pl.BlockSpec((tm, D), lambda i, j: (i, 0)),
                  pl.BlockSpec((1, 1, D), lambda i, j: (mod_row(i), 0, 0)),
                  pl.BlockSpec((1, 1, D), lambda i, j: (mod_row(i), 0, 1)),
                  pl.BlockSpec((1, D), lambda i, j: (0, 0)),
                  pl.BlockSpec((D, tn), lambda i, j: (0, j))],
        out_specs=pl.BlockSpec((tm, tn), lambda i, j: (i, j)),
        out_shape=jax.ShapeDtypeStruct((M, N), F32),
        scratch_shapes=[pltpu.VMEM((tm, D), BF16)],
        compiler_params=_cparams(("parallel", "arbitrary")),
        name="inproj",
    )(x, mods, mods, g_pre, w)


def _outproj_kernel(x_ref, a_ref, b_ref, c_ref, gt_ref, g_ref, wa_ref, wb_ref, wc_ref, o_ref):
    acc = jnp.dot(a_ref[...], wa_ref[...], preferred_element_type=F32)
    acc += jnp.dot(b_ref[...], wb_ref[...], preferred_element_type=F32)
    acc += jnp.dot(c_ref[...].astype(BF16), wc_ref[...], preferred_element_type=F32)
    o_ref[...] = x_ref[...] + gt_ref[0] * _rms(acc, g_ref[...])


def _outproj(x, ma, mb, mc, mods, mod_row, g_post, wa, wb, wc, tm):
    M, D = x.shape
    row = lambda i: (i, 0)
    full = lambda i: (0, 0)
    return pl.pallas_call(
        _outproj_kernel,
        grid=(M // tm,),
        in_specs=[pl.BlockSpec((tm, D), row),
                  pl.BlockSpec((tm, ma.shape[1]), row),
                  pl.BlockSpec((tm, mb.shape[1]), row),
                  pl.BlockSpec((tm, mc.shape[1]), row),
                  pl.BlockSpec((1, 1, D), lambda i: (mod_row(i), 0, 2)),
                  pl.BlockSpec((1, D), full),
                  pl.BlockSpec(wa.shape, full),
                  pl.BlockSpec(wb.shape, full),
                  pl.BlockSpec(wc.shape, full)],
        out_specs=pl.BlockSpec((tm, D), row),
        out_shape=jax.ShapeDtypeStruct((M, D), F32),
        compiler_params=_cparams(("parallel",)),
        name="outproj",
    )(x, ma, mb, mc, mods, g_post, wa, wb, wc)


def _ffn_kernel(x_ref, sh_ref, sc_ref, gt_ref, gpre_ref, gpost_ref, wg_ref, wu_ref, wd_ref,
                o_ref, h_scr, acc_scr):
    j = pl.program_id(1)

    @pl.when(j == 0)
    def _():
        h = _rms(x_ref[...], gpre_ref[...]) * (1.0 + sc_ref[0]) + sh_ref[0]
        h_scr[...] = h.astype(BF16)
        acc_scr[...] = jnp.zeros_like(acc_scr)

    h = h_scr[...]
    g = jnp.dot(h, wg_ref[...], preferred_element_type=F32)
    u = jnp.dot(h, wu_ref[...], preferred_element_type=F32)
    a = (_silu(g) * u).astype(BF16)
    acc_scr[...] += jnp.dot(a, wd_ref[...], preferred_element_type=F32)

    @pl.when(j == pl.num_programs(1) - 1)
    def _():
        o_ref[...] = x_ref[...] + gt_ref[0] * _rms(acc_scr[...], gpost_ref[...])


def _ffn(x, mods, mod_row, g_pre, g_post, wg, wu, wd, tm, th):
    M, D = x.shape
    H = wg.shape[1]
    row = lambda i, j: (i, 0)
    full = lambda i, j: (0, 0)
    return pl.pallas_call(
        _ffn_kernel,
        grid=(M // tm, H // th),
        in_specs=[pl.BlockSpec((tm, D), row),
                  pl.BlockSpec((1, 1, D), lambda i, j: (mod_row(i), 0, 3)),
                  pl.BlockSpec((1, 1, D), lambda i, j: (mod_row(i), 0, 4)),
                  pl.BlockSpec((1, 1, D), lambda i, j: (mod_row(i), 0, 5)),
                  pl.BlockSpec((1, D), full),
                  pl.BlockSpec((1, D), full),
                  pl.BlockSpec((D, th), lambda i, j: (0, j)),
                  pl.BlockSpec((D, th), lambda i, j: (0, j)),
                  pl.BlockSpec((th, D), lambda i, j: (j, 0))],
        out_specs=pl.BlockSpec((tm, D), row),
        out_shape=jax.ShapeDtypeStruct((M, D), F32),
        scratch_shapes=[pltpu.VMEM((tm, D), BF16), pltpu.VMEM((tm, D), F32)],
        compiler_params=_cparams(("parallel", "arbitrary")),
        name="ffn",
    )(x, mods, mods, mods, g_pre, g_post, wg, wu, wd)


def _s5_param_kernel(lre_ref, lim_ref, ldt_ref, bre_ref, bim_ref, cre_ref, cim_ref,
                     k_ref, ca_ref, ab_ref, alre_ref, alim_ref):
    dt = jnp.exp(ldt_ref[...])
    lre, lim = lre_ref[...], lim_ref[...]
    mag = jnp.exp(lre * dt)
    ar = mag * jnp.cos(lim * dt)
    ai = mag * jnp.sin(lim * dt)
    den = lre * lre + lim * lim
    f_re = ((ar - 1.0) * lre + ai * lim) / den
    f_im = (ai * lre - (ar - 1.0) * lim) / den
    b_re, b_im = bre_ref[...], bim_ref[...]
    bb_re = f_re * b_re - f_im * b_im
    bb_im = f_re * b_im + f_im * b_re
    c_re, c_im = cre_ref[...], cim_ref[...]

    def lag_kernel(ca, bb):
        return jnp.einsum('gjp,ghp->gjh', bb, ca, precision=lax.Precision.HIGHEST,
                          preferred_element_type=F32)

    pr, pi = jnp.ones_like(ar), jnp.zeros_like(ar)
    kts = []
    for k in range(S5_CHUNK + 1):
        ca_re = c_re * pr - c_im * pi
        ca_im = c_re * pi + c_im * pr
        ca_ref[k] = jnp.concatenate([ca_re, -ca_im], axis=-1)
        if k < S5_CHUNK:
            kts.append(lag_kernel(ca_re, bb_re) - lag_kernel(ca_im, bb_im))
            ab_ref[k] = jnp.concatenate([pr * bb_re - pi * bb_im, pr * bb_im + pi * bb_re], axis=-1)
        else:
            alre_ref[...] = pr
            alim_ref[...] = pi
        pr, pi = pr * ar - pi * ai, pr * ai + pi * ar
    k_ref[...] = jnp.concatenate(kts, axis=-1)


def _s5_params(lam_re, lam_im, log_dt, bt_re, bt_im, c_re, c_im):
    _, G, P = lam_re.shape
    Hg = bt_re.shape[-2]
    Lc = S5_CHUNK
    gb = 8
    par = lambda *s: pl.BlockSpec((None, gb) + s, lambda d, g: (d, g) + (0,) * len(s))
    lag = lambda n, *s: pl.BlockSpec((None, n, gb) + s, lambda d, g: (d, 0, g) + (0,) * len(s))
    return pl.pallas_call(
        _s5_param_kernel,
        grid=(2, G // gb),
        in_specs=[par(1, P), par(1, P), par(1, 1), par(Hg, P), par(Hg, P), par(Hg, P), par(Hg, P)],
        out_specs=[par(Hg, Lc * Hg), lag(Lc + 1, Hg, 2 * P), lag(Lc, Hg, 2 * P), par(1, P), par(1, P)],
        out_shape=[jax.ShapeDtypeStruct((2, G, Hg, Lc * Hg), F32),
                   jax.ShapeDtypeStruct((2, Lc + 1, G, Hg, 2 * P), F32),
                   jax.ShapeDtypeStruct((2, Lc, G, Hg, 2 * P), F32),
                   jax.ShapeDtypeStruct((2, G, 1, P), F32),
                   jax.ShapeDtypeStruct((2, G, 1, P), F32)],
        compiler_params=_cparams(("parallel", "parallel")),
        name="s5_params",
    )(lam_re.reshape(2, G, 1, P), lam_im.reshape(2, G, 1, P), log_dt.reshape(2, G, 1, 1),
      bt_re, bt_im, c_re, c_im)


def _s5_scan_kernel(xc_ref, xl_ref, kt_ref, ca_ref, ab_ref, alre_ref, alim_ref,
                    yc_ref, yl_ref, u_scr, y_scr, sin_re, sin_im, sst_re, sst_im,
                    *, nb, nc_ctx, nc_lat):
    rc, rl = nb * nc_ctx, nb * nc_lat
    hg, Lc, P = S5_GROUP, S5_CHUNK, S5_STATE
    n_g = xc_ref.shape[-1] // hg
    zero_blk = jnp.zeros((hg, hg), F32)
    for gi in range(n_g):
        lo = gi * hg
        u_scr[0:rc, :] = jnp.concatenate([xc_ref[:, t, lo:lo + hg] for t in range(Lc)], axis=1)
        u_scr[rc:rc + rl, :] = jnp.concatenate([xl_ref[:, t, lo:lo + hg] for t in range(Lc)], axis=1)
        u = u_scr[...].astype(BF16)
        for d in range(2):
            w = jnp.concatenate([ab_ref[d, Lc - 1 - s if d == 0 else s, gi] for s in range(Lc)], axis=0)
            sin = jnp.dot(u, w.astype(BF16), preferred_element_type=F32)
            sin_re[d] = sin[:, :P]
            sin_im[d] = sin[:, P:]
        als = [(alre_ref[d, gi], alim_ref[d, gi]) for d in range(2)]

        def make_body(base, n):
            def body(i, carry):
                nxt = []
                for d in range(2):
                    sr, si = carry[d]
                    ar, ai = als[d]
                    rows = pl.ds(base + (i if d == 0 else n - 1 - i), nb, stride=n)
                    sst_re[d, rows, :] = sr
                    sst_im[d, rows, :] = si
                    nxt.append((ar * sr - ai * si + sin_re[d, rows, :],
                                ar * si + ai * sr + sin_im[d, rows, :]))
                return tuple(nxt)
            return body

        z = jnp.zeros((nb, P), F32)
        carry = lax.fori_loop(0, nc_ctx, make_body(0, nc_ctx), ((z, z), (z, z)))
        lax.fori_loop(0, nc_lat, make_body(rc, nc_lat), carry)
        y = None
        for d in range(2):
            kt = kt_ref[d, gi]
            rows_tt = []
            for s in range(Lc):
                lag = [(t - s) if d == 0 else (s - t) for t in range(Lc)]
                rows_tt.append(jnp.concatenate(
                    [kt[:, k * hg:(k + 1) * hg] if k >= 0 else zero_blk for k in lag], axis=1))
            tt = jnp.concatenate(rows_tt, axis=0)
            o = jnp.concatenate([ca_ref[d, t + 1 if d == 0 else Lc - t, gi] for t in range(Lc)], axis=0)
            sst = jnp.concatenate([sst_re[d], sst_im[d]], axis=1)
            yd = jnp.dot(u, tt.astype(BF16), preferred_element_type=F32)
            yd += _bdot(sst, o, ((1,), (1,)))
            y = yd if y is None else y + yd
        y_scr[gi] = y
    for t in range(Lc):
        piece = jnp.concatenate([y_scr[gi, :, t * hg:(t + 1) * hg] for gi in range(n_g)], axis=1)
        yc_ref[:, t, :] = piece[:rc]
        yl_ref[:, t, :] = piece[rc:]


def _s5_scan(p_ctx, p_lat, B, kt, ca, ab, alre, alim):
    G, P, W, Lc = kt.shape[1], S5_STATE, S5_ROW, S5_CHUNK
    gb = LANES // S5_GROUP
    rc, rl = p_ctx.shape[0] // Lc, p_lat.shape[0] // Lc
    xc = p_ctx.reshape(rc, Lc, p_ctx.shape[1])
    xl = p_lat.reshape(rl, Lc, p_lat.shape[1])
    par = lambda *s: pl.BlockSpec((2, gb) + s, lambda j: (0, j) + (0,) * len(s))
    lag = lambda n, *s: pl.BlockSpec((2, n, gb) + s, lambda j: (0, 0, j) + (0,) * len(s))
    x3 = lambda r: pl.BlockSpec((r, Lc, LANES), lambda j: (0, 0, j))
    R = rc + rl
    yc, yl = pl.pallas_call(
        functools.partial(_s5_scan_kernel, nb=B, nc_ctx=rc // B, nc_lat=rl // B),
        grid=(G // gb,),
        in_specs=[x3(rc), x3(rl), par(S5_GROUP, Lc * S5_GROUP), lag(Lc + 1, S5_GROUP, 2 * P),
                  lag(Lc, S5_GROUP, 2 * P), par(1, P), par(1, P)],
        out_specs=[x3(rc), x3(rl)],
        out_shape=[jax.ShapeDtypeStruct((rc, Lc, G * S5_GROUP), F32),
                   jax.ShapeDtypeStruct((rl, Lc, G * S5_GROUP), F32)],
        scratch_shapes=[pltpu.VMEM((R, W), F32), pltpu.VMEM((gb, R, W), F32)]
                       + [pltpu.VMEM((2, R, P), F32)] * 4,
        compiler_params=_cparams(("parallel",)),
        name="s5_scan",
    )(xc, xl, kt, ca, ab, alre, alim)
    return yc.reshape(rc * Lc, -1), yl.reshape(rl * Lc, -1)


def _s5_out_kernel(y_ref, u_ref, d_ref, w_ref, b_ref, o_ref):
    x = y_ref[...] + d_ref[...] * u_ref[...]
    y = 0.5 * x * (1.0 + jnp.tanh(math.sqrt(2.0 / math.pi) * (x + 0.044715 * (x * x * x))))
    gate = _bdot(y, w_ref[...]) + b_ref[...]
    o_ref[...] = (y * _sigmoid(gate)).astype(o_ref.dtype)


def _s5_out(y, p, d, w, b, tm):
    M, W = y.shape
    row = lambda i: (i, 0)
    full = lambda i: (0, 0)
    return pl.pallas_call(
        _s5_out_kernel,
        grid=(M // tm,),
        in_specs=[pl.BlockSpec((tm, W), row), pl.BlockSpec((tm, W), row),
                  pl.BlockSpec((1, W), full), pl.BlockSpec((W, W), full),
                  pl.BlockSpec((1, W), full)],
        out_specs=pl.BlockSpec((tm, W), row),
        out_shape=jax.ShapeDtypeStruct((M, W), BF16),
        compiler_params=_cparams(("parallel",)),
        name="s5_out",
    )(y, p, d, w, b)


def _s5_mixer(p_ctx, p_lat, B, lam_re, lam_im, log_dt, b_re, b_im, c_re, c_im, d, glu_w, glu_b):
    G, Hg, P, Lc = lam_re.shape[1], S5_GROUP, S5_STATE, S5_CHUNK
    W = G * Hg
    kt, ca, ab, alre, alim = _s5_params(
        lam_re, lam_im, log_dt, b_re.swapaxes(-1, -2), b_im.swapaxes(-1, -2), c_re, c_im)
    y_c, y_l = _s5_scan(p_ctx, p_lat, B, kt, ca, ab, alre, alim)
    dd = d.reshape(1, W)
    gw = glu_w.astype(BF16)
    gb = glu_b.reshape(1, W)
    return (_s5_out(y_c, p_ctx, dd, gw, gb, min(1024, y_c.shape[0])),
            _s5_out(y_l, p_lat, dd, gw, gb, min(1024, y_l.shape[0])))


LANES = 128
GATE_SLOT = 8


def _dot_exact_lhs(b01, a):
    a0, a1, a2 = _split3(a)
    b = b01.astype(BF16)
    d = functools.partial(jnp.dot, preferred_element_type=F32)
    return d(b, a0) + (d(b, a1) + d(b, a2))


def _rt_window(heads):
    start = (6 * heads) // SUBLANES * SUBLANES
    assert 8 * heads <= start + 2 * GATE_SLOT
    return start, 6 * heads - start


def _gates_kernel(xc_ref, xl_ref, pa_ref, pb_ref, gg_ref, gm_ref, cumt_ref, rt_ref, btm_ref,
                  *, tc, tl, heads):
    blk, H = SUPER, heads
    n2 = 2 * H
    per = blk // CHUNK
    cols_per_blk = blk // (tl // GRID_W)
    ri = lax.broadcasted_iota(jnp.int32, (blk, blk), 0)
    ci = lax.broadcasted_iota(jnp.int32, (blk, blk), 1)
    same = (ri // CHUNK) == (ci // CHUNK)
    tri_f = (same & (ri >= ci)).astype(BF16)
    tri_b = (same & (ri <= ci)).astype(BF16)
    fwd_lane = (lax.broadcasted_iota(jnp.int32, (blk, LANES), 1) % n2) < H
    src = lax.broadcasted_iota(jnp.int32, (LANES, LANES), 0)
    dst = lax.broadcasted_iota(jnp.int32, (LANES, LANES), 1)

    def perm(base, q):
        rel = src - base
        tgt = (rel % H) * (2 * GATE_SLOT) + (rel // H) * GATE_SLOT + q
        return ((rel >= 0) & (rel < n2) & (dst == tgt)).astype(BF16)

    i_to_f = ((src >= 2 * n2) & (src < 3 * n2) & (dst == src + n2)).astype(BF16)
    perms_g = [perm(n2, 0)] + [perm(0, q) for q in (1, 2, 3, 4)]
    perms_m = [perm(3 * n2, q) for q in (0, 1, 2)]
    rt0, _ = _rt_window(H)
    pa, pb = pa_ref[...], pb_ref[...]
    zpad = jnp.zeros((blk, LANES - 2 * GATE_SLOT), F32)

    def cumsums(x):
        f = _dot_exact_lhs(tri_f, x)
        b = _dot_exact_lhs(tri_b, x)
        return jnp.where(fwd_lane, f, b), f + b - x

    def place(xs, perms, o_ref):
        out = None
        for x, p in zip(xs, perms):
            t = _dot_exact_rhs(x, p)
            out = t if out is None else out + t
        for h in range(H):
            piece = out[:, h * 2 * GATE_SLOT:(h + 1) * 2 * GATE_SLOT]
            o_ref[:, h * LANES:(h + 1) * LANES] = jnp.concatenate([piece, zpad], axis=1)

    def emit(x_seq, x_scan):
        g = -jnp.exp(pa) * _softplus(x_seq + pb)
        cum, tot = cumsums(g)
        place([_sigmoid(x_seq), cum, jnp.exp(cum), jnp.exp(tot - cum), jnp.exp(tot)], perms_g, gg_ref)
        cumt_ref[...] = cum.T[0:2 * GATE_SLOT, :]
        xb = x_scan + pb
        ig = _dot_exact_rhs(xb, i_to_f)
        b, bt = cumsums(-_softplus(-xb))
        log_w = bt - b + ig
        mch = jnp.max(log_w.reshape(per, CHUNK, LANES), axis=1, keepdims=True)
        ew = jnp.exp(log_w.reshape(per, CHUNK, LANES) - mch).reshape(blk, LANES)
        r = ig - b
        place([b, ew, r], perms_m, gm_ref)
        rt_ref[...] = r.T[rt0:rt0 + 2 * GATE_SLOT, :]
        for j in range(per):
            btm_ref[j, 0:1, :] = bt[j * CHUNK:j * CHUNK + 1, :]
            btm_ref[j, 1:2, :] = mch[j]

    sc = pl.program_id(1)
    n_ctx = tc // blk

    @pl.when(sc < n_ctx)
    def _():
        x = xc_ref[pl.ds(pl.multiple_of(sc * blk, blk), blk), :]
        emit(x, x)

    @pl.when(sc >= n_ctx)
    def _():
        s = sc - n_ctx
        x_seq = xl_ref[pl.ds(pl.multiple_of(s * blk, blk), blk), :]
        x_scan = jnp.concatenate(
            [xl_ref[pl.ds(s * cols_per_blk + j, tl // GRID_W, stride=GRID_W), :]
             for j in range(cols_per_blk)], axis=0)
        emit(x_seq, x_scan)


def _gates(p_ctx, p_lat, B, gate0, pa, pb, heads):
    tc, tl = p_ctx.shape[0] // B, p_lat.shape[0] // B
    tt = tc + tl
    nsc = tt // SUPER
    per = SUPER // CHUNK
    gblk = gate0 // LANES
    tok = pl.BlockSpec((None, SUPER, heads * LANES), lambda b, s: (b, s, 0))
    rowf = pl.BlockSpec((None, None, 2 * GATE_SLOT, SUPER), lambda b, s: (b, s, 0, 0))
    return pl.pallas_call(
        functools.partial(_gates_kernel, tc=tc, tl=tl, heads=heads),
        grid=(B, nsc),
        in_specs=[pl.BlockSpec((tc, LANES), lambda b, s: (b, gblk)),
                  pl.BlockSpec((tl, LANES), lambda b, s: (b, gblk)),
                  pl.BlockSpec((1, LANES), lambda b, s: (0, 0)),
                  pl.BlockSpec((1, LANES), lambda b, s: (0, 0))],
        out_specs=[tok, tok, rowf, rowf,
                   pl.BlockSpec((None, per, 2, LANES), lambda b, s: (b, s, 0, 0))],
        out_shape=[jax.ShapeDtypeStruct((B, tt, heads * LANES), F32),
                   jax.ShapeDtypeStruct((B, tt, heads * LANES), F32),
                   jax.ShapeDtypeStruct((B, nsc, 2 * GATE_SLOT, SUPER), F32),
                   jax.ShapeDtypeStruct((B, nsc, 2 * GATE_SLOT, SUPER), F32),
                   jax.ShapeDtypeStruct((B, tt // CHUNK, 2, LANES), F32)],
        compiler_params=_cparams(("parallel", "parallel")),
        name="gates",
    )(p_ctx, p_lat, pa, pb)


def _gdn_kernel(qc_ref, kc_ref, vc_ref, zc_ref, ql_ref, kl_ref, vl_ref, zl_ref,
                cwq_ref, cwk_ref, cwv_ref, gg_ref, cumt_ref, gn_ref,
                oc_ref, ol_ref, pad_scr, qn, kn, vn, a_scr, b_scr, qp_scr, op_scr, o_scr,
                a2_scr, b2_scr, *, tc, tl, heads):
    dk = HEAD_DIM
    blk = SUPER
    head = pl.program_id(1)

    def conv(src_ref, w_ref, dst, off, t, norm):
        pad_scr[0:8, :] = jnp.zeros((8, dk), F32)
        pad_scr[8:8 + t, :] = src_ref[...]
        pad_scr[8 + t:16 + t, :] = jnp.zeros((8, dk), F32)
        w0, w1, w2 = w_ref[0:1, :], w_ref[1:2, :], w_ref[2:3, :]
        for r0 in range(0, t, blk):
            y = (w0 * pad_scr[r0 + 7:r0 + 7 + blk, :] + w1 * pad_scr[r0 + 8:r0 + 8 + blk, :]
                 + w2 * pad_scr[r0 + 9:r0 + 9 + blk, :])
            y = _silu(y)
            if norm is not None:
                y = y * lax.rsqrt(jnp.sum(y * y, axis=-1, keepdims=True) + EPS) * norm
            dst[off + r0:off + r0 + blk, :] = y

    for src_c, src_l, w_ref, dst, norm in ((qc_ref, ql_ref, cwq_ref, qn, dk ** -0.5),
                                           (kc_ref, kl_ref, cwk_ref, kn, 1.0),
                                           (vc_ref, vl_ref, cwv_ref, vn, None)):
        conv(src_c, w_ref, dst, 0, tc, norm)
        conv(src_l, w_ref, dst, tc, tl, norm)

    n_sc = (tc + tl) // blk
    ncc, ncl = tc // CHUNK, tl // CHUNK
    per = blk // CHUNK
    nt = ((1,), (1,))

    ri = lax.broadcasted_iota(jnp.int32, (blk, blk), 0)
    ci = lax.broadcasted_iota(jnp.int32, (blk, blk), 1)
    same = (ri // CHUNK) == (ci // CHUNK)
    eye_b = (ri == ci).astype(F32)
    tri_m = (same & (ri >= ci), same & (ri <= ci))
    strict_m = (same & (ri > ci), same & (ri < ci))
    lr = lax.broadcasted_iota(jnp.int32, (per * dk, blk), 0)
    lc = lax.broadcasted_iota(jnp.int32, (per * dk, blk), 1)
    chunk_sel = (lr // dk) == (lc // CHUNK)

    def prepare(scs):
        r0s = [pl.multiple_of(sc * blk, blk) for sc in scs]
        qs = [qn[pl.ds(r0, blk), :] for r0 in r0s]
        ks = [kn[pl.ds(r0, blk), :] for r0 in r0s]
        vs = [vn[pl.ds(r0, blk), :] for r0 in r0s]
        qkts = [_bdot(q, k, nt) for q, k in zip(qs, ks)]
        gts = [gg_ref[pl.ds(r0, blk), :] for r0 in r0s]
        items = [(i, d) for i in range(len(scs)) for d in (0, 1)]
        cgs = [gts[i][:, d * GATE_SLOT:(d + 1) * GATE_SLOT] for i, d in items]
        crows = [cumt_ref[scs[i], pl.ds(d * heads + head, 1), :] for i, d in items]
        decs = [jnp.where(tri_m[d], jnp.exp(jnp.where(tri_m[d], cg[:, 1:2] - crow, 0.0)), 0.0)
                for (i, d), cg, crow in zip(items, cgs, crows)]
        kbs = [ks[i] * cg[:, 0:1] for (i, d), cg in zip(items, cgs)]
        ms = [jnp.where(strict_m[d], _bdot(kb, ks[i], nt) * dec, 0.0)
              for (i, d), kb, dec in zip(items, kbs, decs)]
        ps = [eye_b - jnp.where((ri // 2) == (ci // 2), m, 0.0) for m in ms]
        n = 2
        while n < CHUNK:
            join = ((ri // (2 * n)) == (ci // (2 * n))) & ((ri // n) != (ci // n))
            ps = [p - _bdot(p, _bdot(jnp.where(join, m, 0.0), p)) for p, m in zip(ps, ms)]
            n *= 2
        uws = [_bdot(p, jnp.concatenate([vs[i] * cg[:, 0:1], kb * cg[:, 2:3]], axis=1))
               for (i, d), p, cg, kb in zip(items, ps, cgs, kbs)]
        ows = [_bdot(jnp.where(tri_m[d], qkts[i] * dec, 0.0), uw)
               for (i, d), dec, uw in zip(items, decs, uws)]
        lhss = [jnp.where(chunk_sel, jnp.concatenate([(ks[i] * cg[:, 3:4]).T] * per, axis=0), 0.0)
                for (i, d), cg in zip(items, cgs)]
        abs_ = [_bdot(lhs, uw) for lhs, uw in zip(lhss, uws)]
        for (i, d), cg, ow, ab in zip(items, cgs, ows, abs_):
            ra = pl.multiple_of(scs[i] * (per * dk), per * dk)
            qp_scr[d, pl.ds(r0s[i], blk), :] = (qs[i] * cg[:, 2:3] - ow[:, dk:]).astype(BF16)
            op_scr[d, pl.ds(r0s[i], blk), :] = ow[:, :dk]
            a_scr[d, pl.ds(ra, per * dk), :] = ab[:, dk:].astype(BF16)
            b_scr[d, pl.ds(ra, per * dk), :] = ab[:, :dk]
            for m in range(per // 2):
                j0, j1 = (2 * m, 2 * m + 1) if d == 0 else (2 * m + 1, 2 * m)
                e0 = cg[j0 * CHUNK:j0 * CHUNK + 1, 4:5]
                e1 = cg[j1 * CHUNK:j1 * CHUNK + 1, 4:5]
                ab0 = ab[j0 * dk:(j0 + 1) * dk, :]
                ab1 = ab[j1 * dk:(j1 + 1) * dk, :]
                x = _bdot(ab1[:, dk:], ab0)
                rp = pl.multiple_of((scs[i] * (per // 2) + m) * dk, dk)
                a2_scr[d, pl.ds(rp, dk), :] = (e1 * ab0[:, dk:] + e0 * ab1[:, dk:] - x[:, dk:]).astype(BF16)
                b2_scr[d, pl.ds(rp, dk), :] = e1 * ab0[:, :dk] + ab1[:, :dk] - x[:, :dk]

    def prepare_one(sc, _):
        prepare((sc,))
        return 0

    def prepare_two(i, _):
        prepare((tc // blk + 2 * i, tc // blk + 2 * i + 1))
        return 0

    n_lat = tl // blk
    lax.fori_loop(0, tc // blk, prepare_one, 0)
    lax.fori_loop(0, n_lat // 2, prepare_two, 0)
    if n_lat % 2:
        prepare_one(n_sc - 1, 0)

    def make_step(n, off_p):
        def step(i, carry):
            nxt = []
            for d in range(2):
                s = carry[d]
                p = off_p + (i if d == 0 else n - 1 - i)
                sb = s.astype(BF16)

                def emit(c, state_bf16):
                    r0 = pl.multiple_of(c * CHUNK, CHUNK)
                    o = jnp.dot(qp_scr[d, pl.ds(r0, CHUNK), :], state_bf16, preferred_element_type=F32)
                    o_scr[d, pl.ds(r0, CHUNK), :] = o + op_scr[d, pl.ds(r0, CHUNK), :]
                    return gg_ref[pl.ds(r0, 1), :][:, d * GATE_SLOT + 4:d * GATE_SLOT + 5]

                c0, c1 = 2 * p + d, 2 * p + 1 - d
                e0 = emit(c0, sb)
                ra = pl.multiple_of(c0 * dk, dk)
                s1 = (s * e0 + b_scr[d, pl.ds(ra, dk), :]
                      - jnp.dot(a_scr[d, pl.ds(ra, dk), :], sb, preferred_element_type=F32))
                e1 = emit(c1, s1.astype(BF16))
                rp = pl.multiple_of(p * dk, dk)
                nxt.append(s * (e0 * e1) + b2_scr[d, pl.ds(rp, dk), :]
                           - jnp.dot(a2_scr[d, pl.ds(rp, dk), :], sb, preferred_element_type=F32))
            return tuple(nxt)
        return step

    s0 = jnp.zeros((dk, dk), F32)
    carry = lax.fori_loop(0, ncc // 2, make_step(ncc // 2, 0), (s0, s0), unroll=2)
    lax.fori_loop(0, ncl // 2, make_step(ncl // 2, ncc // 2), carry, unroll=2)

    gn = gn_ref[...]
    for z_ref, o_ref, off, t in ((zc_ref, oc_ref, 0, tc), (zl_ref, ol_ref, tc, tl)):
        for r0 in range(0, t, blk):
            o = o_scr[0, off + r0:off + r0 + blk, :] + o_scr[1, off + r0:off + r0 + blk, :]
            o_ref[r0:r0 + blk, :] = (_rms(o, gn) * _silu(z_ref[r0:r0 + blk, :])).astype(o_ref.dtype)


def _gdn(p_ctx, p_lat, B, conv_w, gg, cumt, gnorm, H):
    tc, tl = p_ctx.shape[0] // B, p_lat.shape[0] // B
    tt = tc + tl
    q0, k0, v0, z0 = 4, 4 + H, 4 + 2 * H, 4 + 3 * H

    def col(t, c0):
        return pl.BlockSpec((t, HEAD_DIM), lambda b, h: (b, c0 + h))

    return pl.pallas_call(
        functools.partial(_gdn_kernel, tc=tc, tl=tl, heads=H),
        grid=(B, H),
        in_specs=[col(tc, q0), col(tc, k0), col(tc, v0), col(tc, z0),
                  col(tl, q0), col(tl, k0), col(tl, v0), col(tl, z0),
                  pl.BlockSpec((3, HEAD_DIM), lambda b, h: (0, h)),
                  pl.BlockSpec((3, HEAD_DIM), lambda b, h: (0, H + h)),
                  pl.BlockSpec((3, HEAD_DIM), lambda b, h: (0, 2 * H + h)),
                  pl.BlockSpec((None, tt, LANES), lambda b, h: (b, 0, h)),
                  pl.BlockSpec((None,) + cumt.shape[1:], lambda b, h: (b, 0, 0, 0)),
                  pl.BlockSpec((1, HEAD_DIM), lambda b, h: (0, 0))],
        out_specs=[pl.BlockSpec((tc, HEAD_DIM), lambda b, h: (b, h)),
                   pl.BlockSpec((tl, HEAD_DIM), lambda b, h: (b, h))],
        out_shape=[jax.ShapeDtypeStruct((B * tc, H * HEAD_DIM), BF16),
                   jax.ShapeDtypeStruct((B * tl, H * HEAD_DIM), BF16)],
        scratch_shapes=[pltpu.VMEM((max(tc, tl) + 16, HEAD_DIM), F32)]
                       + [pltpu.VMEM((tt, HEAD_DIM), F32)] * 3
                       + [pltpu.VMEM((2, tt // CHUNK * HEAD_DIM, HEAD_DIM), BF16),
                          pltpu.VMEM((2, tt // CHUNK * HEAD_DIM, HEAD_DIM), F32),
                          pltpu.VMEM((2, tt, HEAD_DIM), BF16),
                          pltpu.VMEM((2, tt, HEAD_DIM), F32),
                          pltpu.VMEM((2, tt, HEAD_DIM), F32),
                          pltpu.VMEM((2, tt // (2 * CHUNK) * HEAD_DIM, HEAD_DIM), BF16),
                          pltpu.VMEM((2, tt // (2 * CHUNK) * HEAD_DIM, HEAD_DIM), F32)],
        compiler_params=_cparams(("parallel", "parallel")),
        name="gdn",
    )(p_ctx, p_ctx, p_ctx, p_ctx, p_lat, p_lat, p_lat, p_lat, conv_w, conv_w, conv_w,
      gg, cumt, gnorm)


def _mlstm_kernel(qkc_ref, vc_ref, oc_ref, qkl_ref, vl_ref, ol_ref, gm_ref, rt_ref,
                  btm_ref, gn_ref, hc_ref, hl_ref, na_scr, kva_scr, qs_scr, ms_scr, h_scr, sc_scr,
                  *, tc, tl, heads):
    blk, dk, dv = SUPER, ML_DK, HEAD_DIM
    per = blk // CHUNK
    ncc, ncl = tc // CHUNK, tl // CHUNK
    qscale = dk ** -0.5
    head = pl.program_id(1)
    rt_off = _rt_window(heads)[1]
    grid_rows = tl // GRID_W
    cols_per_blk = blk // grid_rows

    def seq_rows(ref, s):
        return ref[pl.ds(pl.multiple_of(s * blk, blk), blk), :]

    def grid_col_rows(ref, s):
        return jnp.concatenate([ref[pl.ds(s * cols_per_blk + j, grid_rows, stride=GRID_W), :]
                                for j in range(cols_per_blk)], axis=0)

    lane_n = lax.broadcasted_iota(jnp.int32, (ncc + ncl, LANES), 1)
    for d in range(2):
        pick = lane_n == 6 * heads + d * heads + head
        for j in range(2):
            col = jnp.sum(jnp.where(pick, btm_ref[:, j, :], 0.0), axis=1, keepdims=True)
            sc_scr[d, j] = jnp.broadcast_to(col, (ncc + ncl, 2 * dv))

    ri = lax.broadcasted_iota(jnp.int32, (blk, blk), 0)
    ci = lax.broadcasted_iota(jnp.int32, (blk, blk), 1)
    same = (ri // CHUNK) == (ci // CHUNK)
    tri_m = (same & (ri >= ci), same & (ri <= ci))
    lane2 = lax.broadcasted_iota(jnp.int32, (blk, 2 * dv), 1)
    one_col = (lax.broadcasted_iota(jnp.int32, (blk, dv), 1) == 0).astype(F32)

    for d in range(2):
        def make_scalar_step(n, off_c):
            def step(i, m_st):
                c = off_c + (i if d == 0 else n - 1 - i)
                bt = sc_scr[d, 0, pl.ds(c, 1), :]
                mch = sc_scr[d, 1, pl.ds(c, 1), :]
                m_new = jnp.maximum(bt + m_st, mch)
                ms_scr[d, c, 0:1, :] = m_st
                ms_scr[d, c, 1:2, :] = jnp.exp(bt + m_st - m_new)
                ms_scr[d, c, 2:3, :] = jnp.exp(mch - m_new)
                return m_new
            return step
        m_c = lax.fori_loop(0, ncc, make_scalar_step(ncc, 0), jnp.zeros((1, 2 * dv), F32))
        lax.fori_loop(0, ncl, make_scalar_step(ncl, ncc), m_c)

    rblk = lax.broadcasted_iota(jnp.int32, (blk, 1), 0) // CHUNK

    def per_chunk_col(vals):
        col = vals[per - 1]
        for j in range(per - 2, -1, -1):
            col = jnp.where(rblk == j, vals[j], col)
        return col

    def prepare(qk_ref, v_ref, off_sc, rows, ss):
        scs = [off_sc + s for s in ss]
        r0s = [pl.multiple_of(sc * blk, blk) for sc in scs]
        qks = [rows(qk_ref, s) for s in ss]
        qs = [qk[:, :dk] * qscale for qk in qks]
        ks = [qk[:, dk:] for qk in qks]
        vas = [jnp.concatenate([rows(v_ref, s), one_col], axis=1) for s in ss]
        qkts = [_bdot(q, k, ((1,), (1,))) for q, k in zip(qs, ks)]
        cms = [gm_ref[pl.ds(r0, blk), :] for r0 in r0s]
        items = [(i, d) for i in range(len(ss)) for d in (0, 1)]
        b_cols = [cms[i][:, d * GATE_SLOT:d * GATE_SLOT + 1] for i, d in items]
        ews = [cms[i][:, d * GATE_SLOT + 1:d * GATE_SLOT + 2] for i, d in items]
        r_rows = [rt_ref[scs[i], pl.ds(rt_off + d * heads + head, 1), :] for i, d in items]
        mss = [[ms_scr[d, scs[i] * per + j] for j in range(per)] for i, d in items]
        m_sts = [per_chunk_col([m[0:1, 0:1] for m in ms]) for ms in mss]
        a_news = [per_chunk_col([m[2:3, 0:1] for m in ms]) for ms in mss]
        log_ds = [jnp.where(tri_m[d], b + r, -jnp.inf) for (i, d), b, r in zip(items, b_cols, r_rows)]
        m_ts = [jnp.maximum(b + m, jnp.max(ld, axis=-1, keepdims=True))
                for b, m, ld in zip(b_cols, m_sts, log_ds)]
        nas = [_bdot(qkts[i] * jnp.exp(ld - mt), vas[i])
               for (i, d), ld, mt in zip(items, log_ds, m_ts)]
        lhss = [jnp.where(same, jnp.concatenate([(ks[i] * (ew * an)).T] * per, axis=0), 0.0)
                for (i, d), ew, an in zip(items, ews, a_news)]
        kvas = [_bdot(lhs, vas[i]) for (i, d), lhs in zip(items, lhss)]
        for n, (i, d) in enumerate(items):
            na_scr[d, pl.ds(r0s[i], blk), :] = jnp.where(lane2 == dv + 1, jnp.exp(-m_ts[n]), nas[n])
            qs_scr[d, pl.ds(r0s[i], blk), :] = (
                qs[i] * jnp.exp(b_cols[n] + m_sts[n] - m_ts[n])).astype(BF16)
            kva_scr[d, pl.ds(r0s[i], blk), :] = kvas[n]

    def prepare_ctx(s, _):
        prepare(qkc_ref, vc_ref, 0, seq_rows, (s,))
        return 0

    def prepare_lat_pair(i, _):
        prepare(qkl_ref, vl_ref, tc // blk, grid_col_rows, (2 * i, 2 * i + 1))
        return 0

    n_lat = tl // blk
    lax.fori_loop(0, tc // blk, prepare_ctx, 0)
    lax.fori_loop(0, n_lat // 2, prepare_lat_pair, 0)
    if n_lat % 2:
        prepare(qkl_ref, vl_ref, tc // blk, grid_col_rows, (n_lat - 1,))

    def make_step(n, off_c):
        def step(i, carry):
            nxt = []
            for d in range(2):
                c_st = carry[d]
                c = off_c + (i if d == 0 else n - 1 - i)
                r0 = pl.multiple_of(c * CHUNK, CHUNK)
                na = na_scr[d, pl.ds(r0, CHUNK), :]
                tot = na + jnp.dot(qs_scr[d, pl.ds(r0, CHUNK), :], c_st.astype(BF16),
                                   preferred_element_type=F32)
                den = jnp.maximum(jnp.abs(tot[:, dv:dv + 1]), na[:, dv + 1:dv + 2])
                h_scr[d, pl.ds(r0, CHUNK), :] = tot[:, :dv] / den
                nxt.append(ms_scr[d, c, 1:2, :] * c_st + kva_scr[d, pl.ds(r0, CHUNK), :])
            return tuple(nxt)
        return step

    st0 = jnp.zeros((dk, 2 * dv), F32)
    carry = lax.fori_loop(0, ncc, make_step(ncc, 0), (st0, st0), unroll=4)
    lax.fori_loop(0, ncl, make_step(ncl, ncc), carry, unroll=4)

    gn = gn_ref[...]

    def gated(r0, og):
        h = h_scr[0, r0:r0 + blk, :] + h_scr[1, r0:r0 + blk, :]
        return (_rms(h, gn) * _sigmoid(og)).astype(hc_ref.dtype)

    for s in range(tc // blk):
        hc_ref[s * blk:(s + 1) * blk, :] = gated(s * blk, oc_ref[s * blk:(s + 1) * blk, :])
    for s in range(tl // blk):
        res = gated(tc + s * blk, grid_col_rows(ol_ref, s))
        for j in range(cols_per_blk):
            hl_ref[pl.ds(s * cols_per_blk + j, grid_rows, stride=GRID_W), :] = (
                res[j * grid_rows:(j + 1) * grid_rows])


def _mlstm(p_ctx, p_lat, B, gm, rt, btm, gnorm, qk0, H):
    tc, tl = p_ctx.shape[0] // B, p_lat.shape[0] // B
    tt = tc + tl

    def col(t, c0):
        return pl.BlockSpec((t, HEAD_DIM), lambda b, h: (b, c0 + h))

    return pl.pallas_call(
        functools.partial(_mlstm_kernel, tc=tc, tl=tl, heads=H),
        grid=(B, H),
        in_specs=[col(tc, qk0), col(tc, qk0 + H), col(tc, qk0 + 2 * H),
                  col(tl, qk0), col(tl, qk0 + H), col(tl, qk0 + 2 * H),
                  pl.BlockSpec((None, tt, LANES), lambda b, h: (b, 0, h)),
                  pl.BlockSpec((None,) + rt.shape[1:], lambda b, h: (b, 0, 0, 0)),
                  pl.BlockSpec((None,) + btm.shape[1:], lambda b, h: (b, 0, 0, 0)),
                  pl.BlockSpec((1, HEAD_DIM), lambda b, h: (0, 0))],
        out_specs=[pl.BlockSpec((tc, HEAD_DIM), lambda b, h: (b, h)),
                   pl.BlockSpec((tl, HEAD_DIM), lambda b, h: (b, h))],
        out_shape=[jax.ShapeDtypeStruct((B * tc, H * HEAD_DIM), F32),
                   jax.ShapeDtypeStruct((B * tl, H * HEAD_DIM), F32)],
        scratch_shapes=[pltpu.VMEM((2, tt, 2 * HEAD_DIM), F32)] * 2
                       + [pltpu.VMEM((2, tt, ML_DK), BF16),
                          pltpu.VMEM((2, tt // CHUNK, 3, 2 * HEAD_DIM), F32),
                          pltpu.VMEM((2, tt, HEAD_DIM), F32),
                          pltpu.VMEM((2, 2, tt // CHUNK, 2 * HEAD_DIM), F32)],
        compiler_params=_cparams(("parallel", "parallel")),
        name="mlstm",
    )(p_ctx, p_ctx, p_ctx, p_lat, p_lat, p_lat, gm, rt, btm, gnorm)


def _in_sizes(D):
    s5 = D // 4
    gw = 3 * D // 8
    gh = gw // HEAD_DIM
    mw = D - s5 - gw
    mh = mw // HEAD_DIM
    return s5, gw, gh, mw, mh


def _permute_w_in(w, D):
    s5, gw, gh, mw, mh = _in_sizes(D)
    o_a = s5 + 4 * gw
    o_mq = o_a + 4 * gh
    o_mk = o_mq + mh * ML_DK
    o_mv = o_mk + mh * ML_DK
    o_mi = o_mv + 2 * mw
    end = o_mi + 4 * mh
    pieces = [w[:, :o_a]]
    for h in range(mh):
        pieces += [w[:, o_mq + h * ML_DK:o_mq + (h + 1) * ML_DK],
                   w[:, o_mk + h * ML_DK:o_mk + (h + 1) * ML_DK]]
    pieces += [w[:, o_mv:o_mi], w[:, o_a:o_mq], w[:, o_mi:end]]
    used = o_a + 2 * mh * ML_DK + 2 * mw + 4 * gh + 4 * mh
    total = -(-used // 512) * 512
    pieces.append(jnp.zeros((w.shape[0], total - used), w.dtype))
    gate0 = o_a + 2 * mh * ML_DK + 2 * mw
    return jnp.concatenate(pieces, axis=1).astype(BF16), gate0


def _token_mixer(p_ctx, p_lat, B, gate0, prm):
    (s5_lam_re, s5_lam_im, s5_log_dt, s5_b_re, s5_b_im, s5_c_re, s5_c_im, s5_d, s5_glu_w,
     s5_glu_b, gdn_conv_w, gdn_a_log, gdn_dt_bias, gdn_norm, ml_i_bias, ml_f_bias, ml_norm) = prm
    tc, tl = p_ctx.shape[0] // B, p_lat.shape[0] // B
    rows = tl // GRID_W
    H = gdn_a_log.shape[-1]
    nch = (tc + tl) // CHUNK

    s5_c, s5_l = _s5_mixer(p_ctx, p_lat, B, s5_lam_re, s5_lam_im, s5_log_dt, s5_b_re, s5_b_im,
                           s5_c_re, s5_c_im, s5_d, s5_glu_w, s5_glu_b)

    n2 = 2 * H
    zero = jnp.zeros((n2,), F32)
    tail = jnp.zeros((LANES - 4 * n2,), F32)
    pa = jnp.concatenate([gdn_a_log.reshape(-1), zero, zero, zero, tail]).reshape(1, LANES)
    pb = jnp.concatenate([gdn_dt_bias.reshape(-1), zero, ml_i_bias.reshape(-1),
                          ml_f_bias.reshape(-1), tail]).reshape(1, LANES)
    gg, gm, cumt, rt, btm = _gates(p_ctx, p_lat, B, gate0, pa, pb, H)

    gd_c, gd_l = _gdn(p_ctx, p_lat, B, gdn_conv_w, gg, cumt, gdn_norm.reshape(1, HEAD_DIM), H)
    qk0 = gate0 // HEAD_DIM - 3 * H
    ml_c, ml_l = _mlstm(p_ctx, p_lat, B, gm, rt, btm, ml_norm.reshape(1, HEAD_DIM), qk0, H)
    return (s5_c, gd_c, ml_c), (s5_l, gd_l, ml_l)


def kernel(x, c, ctx, c_ctx, ada_w, ada_b, norm_mix_pre, norm_mix_post, norm_ffn_pre, norm_ffn_post, w_in, w_out, s5_lam_re, s5_lam_im, s5_log_dt, s5_b_re, s5_b_im, s5_c_re, s5_c_im, s5_d, s5_glu_w, s5_glu_b, gdn_conv_w, gdn_a_log, gdn_dt_bias, gdn_norm, mlstm_i_bias, mlstm_f_bias, mlstm_norm, ffn_w_gate, ffn_w_up, ffn_w_down):
    B, T, D = x.shape
    TC = ctx.shape[1]
    L = ada_w.shape[0]
    assert T % SUPER == 0 and TC % SUPER == 0 and T % GRID_W == 0 and B <= SUBLANES - 1
    s5w, gw, _, _, _ = _in_sizes(D)

    x_lat = x.reshape(B * T, D)
    x_ctx = ctx.reshape(B * TC, D)
    c8 = jnp.concatenate([c, c_ctx[None], jnp.zeros((SUBLANES - B - 1, D), F32)], 0)
    mods = _ada(c8, ada_w, ada_b).reshape(L * SUBLANES, 1, 6 * D)

    tm_lat = 1024
    tm_ctx = min(1024, B * TC)
    tiles_per_batch = T // tm_lat

    for l in range(L):
        lat_row = lambda i, l=l: l * SUBLANES + i // tiles_per_batch
        ctx_row = lambda i, l=l: l * SUBLANES + B
        g_mix_pre = norm_mix_pre[l].reshape(1, D)
        g_mix_post = norm_mix_post[l].reshape(1, D)
        g_ffn_pre = norm_ffn_pre[l].reshape(1, D)
        g_ffn_post = norm_ffn_post[l].reshape(1, D)

        w_in_p, gate0 = _permute_w_in(w_in[l], D)
        p_lat = _inproj(x_lat, mods, lat_row, g_mix_pre, w_in_p, tm_lat, 1024)
        p_ctx = _inproj(x_ctx, mods, ctx_row, g_mix_pre, w_in_p, tm_ctx, 1024)

        prm = (s5_lam_re[l], s5_lam_im[l], s5_log_dt[l], s5_b_re[l], s5_b_im[l], s5_c_re[l],
               s5_c_im[l], s5_d[l], s5_glu_w[l], s5_glu_b[l], gdn_conv_w[l], gdn_a_log[l],
               gdn_dt_bias[l], gdn_norm[l], mlstm_i_bias[l], mlstm_f_bias[l], mlstm_norm[l])
        mix_ctx, mix_lat = _token_mixer(p_ctx, p_lat, B, gate0, prm)

        wo = _to_bf16(w_out, l)
        wa, wb, wc = wo[:s5w], wo[s5w:s5w + gw], wo[s5w + gw:]
        wg = _to_bf16(ffn_w_gate, l)
        wu = _to_bf16(ffn_w_up, l)
        wd = _to_bf16(ffn_w_down, l)

        lat_row_f = lambda i, l=l: l * SUBLANES + i // (T // 512)
        xs = _outproj(x_lat, mix_lat[0], mix_lat[1], mix_lat[2], mods, lat_row_f, g_mix_post,
                      wa, wb, wc, 512)
        x_lat = _ffn(xs, mods, lat_row_f, g_ffn_pre, g_ffn_post, wg, wu, wd, 512, 512)
        if l < L - 1:
            xs = _outproj(x_ctx, mix_ctx[0], mix_ctx[1], mix_ctx[2], mods, ctx_row, g_mix_post,
                          wa, wb, wc, 512)
            x_ctx = _ffn(xs, mods, ctx_row, g_ffn_pre, g_ffn_post, wg, wu, wd, 512, 512)
    return x_lat.reshape(B, T, D)
```
